```python
import math
import jax
import jax.numpy as jnp
from jax import lax
import numpy as np

D_MODEL = 4096
BATCH = 32
SEQ = 256
DEPTH = 2
DEC_BATCH = 2
DEC_SEQ = 2048
PAST_LEN = 256

GRID_W = 64
Q_BLOCK = 128
ROPE_THETA = 10000.0
EPS = 1e-6

MLA_HEADS = 8
MLA_NOPE = 128
MLA_ROPE = 64
MLA_V = 128
MLA_QK = MLA_NOPE + MLA_ROPE
Q_LORA = 768
KV_LORA = 512
CONV_CH = 1024
CONV_K = 31
GQA_HEADS = 8
GQA_KV_HEADS = 2
GQA_HD = 128
DN_HEADS = 8
DN_DK = 128
DN_DV = 128
DN_CONV = 5
DN_CHUNK = 64
N_BRANCH = 4
BRANCH_W = 1024
N_EXPERTS = 64
TOP_K = 8
EXPERT_F = 512
SHARED_F = 512
ROUTE_SCALE = 2.5
MOE_BLOCK = 128

IN_SIZES = (Q_LORA, KV_LORA, MLA_ROPE, 2 * CONV_CH,
            GQA_HEADS * GQA_HD, GQA_KV_HEADS * GQA_HD, GQA_KV_HEADS * GQA_HD,
            DN_HEADS * DN_DK, DN_HEADS * DN_DK, DN_HEADS * DN_DV, DN_HEADS * DN_DV,
            2 * DN_HEADS, 2 * DN_HEADS, N_BRANCH * D_MODEL)

kernel_name = "hybrid_diffusion_prefix_trunk"


def rms_norm(x, w):
    xf = x.astype(jnp.float32)
    y = xf * lax.rsqrt(jnp.mean(xf * xf, axis=-1, keepdims=True) + EPS)
    return (y * w.astype(jnp.float32)).astype(x.dtype)


def layer_norm(x, w, b):
    xf = x.astype(jnp.float32)
    xc = xf - jnp.mean(xf, axis=-1, keepdims=True)
    y = xc * lax.rsqrt(jnp.mean(xc * xc, axis=-1, keepdims=True) + EPS)
    return (y * w.astype(jnp.float32) + b.astype(jnp.float32)).astype(x.dtype)


def l2_normalize(x):
    return x * lax.rsqrt(jnp.sum(x * x, axis=-1, keepdims=True) + EPS)


def split_cols(p):
    return jnp.split(p, np.cumsum(IN_SIZES)[:-1].tolist(), axis=-1)


def axial_rope(x):
    n_tok, n_rot = x.shape[1], x.shape[-1]
    rows = n_tok // GRID_W
    row = jnp.repeat(jnp.arange(rows, dtype=jnp.float32), GRID_W)
    col = jnp.broadcast_to(jnp.arange(GRID_W, dtype=jnp.float32), (rows, GRID_W)).reshape(-1)
    half = n_rot // 2
    inv_freq = ROPE_THETA ** (-jnp.arange(0, half, 2, dtype=jnp.float32) / half)

    def rotate(xa, pos):
        ang = pos[:, None] * inv_freq[None, :]
        cos = jnp.cos(ang)[None, :, None, :]
        sin = jnp.sin(ang)[None, :, None, :]
        x1, x2 = jnp.split(xa, 2, axis=-1)
        return jnp.concatenate([x1 * cos - x2 * sin, x2 * cos + x1 * sin], axis=-1)

    xf = x.astype(jnp.float32)
    out = jnp.concatenate([rotate(xf[..., :half], row), rotate(xf[..., half:], col)], axis=-1)
    return out.astype(x.dtype)


def depthwise_conv(x, w):
    width, ch = w.shape
    pad = width // 2
    return lax.conv_general_dilated(x, w[:, None, :].astype(x.dtype), (1,), [(pad, pad)],
                                    dimension_numbers=("NWC", "WIO", "NWC"), feature_group_count=ch)


def blocked_attention(q, k, v, scale):
    b, tq, hq, dk = q.shape
    hk, dv = k.shape[2], v.shape[-1]
    grp = hq // hk
    nb = tq // Q_BLOCK
    qb = jnp.moveaxis(q.reshape(b, nb, Q_BLOCK, hk, grp, dk), 1, 0)

    def attend(q_blk):
        s = jnp.einsum("bqhgd,bkhd->bhgqk", q_blk, k, preferred_element_type=jnp.float32) * scale
        p = jax.nn.softmax(s, axis=-1).astype(v.dtype)
        return jnp.einsum("bhgqk,bkhd->bqhgd", p, v)

    o = lax.map(attend, qb)
    return jnp.moveaxis(o, 0, 1).reshape(b, tq, hq, dv)


def conformer_conv(glu_in, dw_w, dw_b, ln_w, ln_b):
    a, g = jnp.split(glu_in, 2, axis=-1)
    y = depthwise_conv(a * jax.nn.sigmoid(g), dw_w) + dw_b
    return jax.nn.silu(layer_norm(y, ln_w, ln_b))


def gated_delta_chunked(q, k, v, g, beta, s0):
    b, t, h, _ = q.shape
    dv = v.shape[-1]
    n = t // DN_CHUNK

    def to_chunks(a):
        a = a.reshape((b, n, DN_CHUNK, h) + a.shape[3:])
        return jnp.moveaxis(jnp.moveaxis(a, 1, 0), 3, 2)

    qc, kc, vc, bc = to_chunks(q), to_chunks(k), to_chunks(v), to_chunks(beta)
    gc = jnp.cumsum(to_chunks(g), axis=-1)
    pos = jnp.arange(DN_CHUNK)
    incl = pos[:, None] >= pos[None, :]
    strict = pos[:, None] > pos[None, :]
    decay = jnp.exp(jnp.where(incl, gc[..., :, None] - gc[..., None, :], -jnp.inf))
    kb = kc * bc[..., None]
    lmat = jnp.where(strict, jnp.einsum("...id,...jd->...ij", kb, kc) * decay, 0.0)
    rhs = jnp.concatenate([vc * bc[..., None], kb * jnp.exp(gc)[..., None]], axis=-1)
    sol = lax.linalg.triangular_solve(lmat, rhs, left_side=True, lower=True, unit_diagonal=True)
    u, w = sol[..., :dv], sol[..., dv:]
    qk = jnp.einsum("...id,...jd->...ij", qc, kc) * decay
    q_dec = qc * jnp.exp(gc)[..., None]
    k_dec = kc * jnp.exp(gc[..., -1:] - gc)[..., None]
    g_end = jnp.exp(gc[..., -1])

    def step(s, xs):
        qk_i, qd_i, kd_i, u_i, w_i, ge_i = xs
        delta = u_i - jnp.einsum("bhck,bhkv->bhcv", w_i, s)
        o_i = jnp.einsum("bhck,bhkv->bhcv", qd_i, s) + jnp.einsum("bhcj,bhjv->bhcv", qk_i, delta)
        s = s * ge_i[..., None, None] + jnp.einsum("bhck,bhcv->bhkv", kd_i, delta)
        return s, o_i

    s_fin, o = lax.scan(step, s0, (qk, q_dec, k_dec, u, w, g_end))
    o = jnp.moveaxis(jnp.moveaxis(o, 2, 3), 0, 1).reshape(b, t, h, dv)
    return o, s_fin


def deltanet_branch(d_q, d_k, d_v, d_z, d_a, d_b, conv_w, a_log, dt_bias, o_norm, s0):
    b, t, _ = d_q.shape
    dtype = d_q.dtype
    qkv = jax.nn.silu(depthwise_conv(jnp.concatenate([d_q, d_k, d_v], axis=-1), conv_w)).astype(jnp.float32)
    q, k, v = jnp.split(qkv, [DN_HEADS * DN_DK, 2 * DN_HEADS * DN_DK], axis=-1)
    q = l2_normalize(q.reshape(b, t, DN_HEADS, DN_DK)) * DN_DK ** -0.5
    k = l2_normalize(k.reshape(b, t, DN_HEADS, DN_DK))
    v = v.reshape(b, t, DN_HEADS, DN_DV)
    beta = jax.nn.sigmoid(d_b.astype(jnp.float32)).reshape(b, t, 2, DN_HEADS)
    g = -jnp.exp(a_log.astype(jnp.float32)) * jax.nn.softplus(
        d_a.astype(jnp.float32).reshape(b, t, 2, DN_HEADS) + dt_bias.astype(jnp.float32))
    if s0 is None:
        s0 = jnp.zeros((b, 2, DN_HEADS, DN_DK, DN_DV), jnp.float32)
    s0 = s0.astype(jnp.float32)

    def rev(a):
        return jnp.flip(a, axis=1)

    o_f, s_f = gated_delta_chunked(q, k, v, g[:, :, 0], beta[:, :, 0], s0[:, 0])
    o_b, s_b = gated_delta_chunked(rev(q), rev(k), rev(v), rev(g[:, :, 1]), rev(beta[:, :, 1]), s0[:, 1])
    o = o_f + rev(o_b)
    o = rms_norm(o, o_norm) * jax.nn.silu(d_z.astype(jnp.float32).reshape(b, t, DN_HEADS, DN_DV))
    return o.reshape(b, t, DN_HEADS * DN_DV).astype(dtype), jnp.stack([s_f, s_b], axis=1).astype(dtype)


def swiglu(x, w_gu, w_down):
    g, u = jnp.split(x @ w_gu, 2, axis=-1)
    return (jax.nn.silu(g) * u) @ w_down


def routed_experts(x, idx, wts, w_gu, w_down):
    n_asg = x.shape[0] * TOP_K
    flat_e = idx.reshape(n_asg)
    flat_tok = jnp.arange(n_asg, dtype=jnp.int32) // TOP_K
    flat_w = wts.reshape(n_asg)
    order = jnp.argsort(flat_e)
    e_sorted = flat_e[order]
    counts = jnp.bincount(flat_e, length=N_EXPERTS)
    padded = (counts + MOE_BLOCK - 1) // MOE_BLOCK * MOE_BLOCK
    pad_end = jnp.cumsum(padded)
    pad_start = pad_end - padded
    start = jnp.cumsum(counts) - counts
    dest = pad_start[e_sorted] + jnp.arange(n_asg, dtype=jnp.int32) - start[e_sorted]
    n_blocks = (n_asg + N_EXPERTS * (MOE_BLOCK - 1) + MOE_BLOCK - 1) // MOE_BLOCK
    n_slots = n_blocks * MOE_BLOCK
    slot_tok = jnp.zeros((n_slots,), jnp.int32).at[dest].set(flat_tok[order])
    slot_w = jnp.zeros((n_slots,), x.dtype).at[dest].set(flat_w[order])
    block_e = jnp.minimum(jnp.searchsorted(pad_end, jnp.arange(n_blocks) * MOE_BLOCK, side="right"),
                          N_EXPERTS - 1)

    def body(y, blk):
        tok, wt, e = blk
        g, u = jnp.split(x[tok] @ w_gu[e], 2, axis=-1)
        yb = (jax.nn.silu(g) * u) @ w_down[e]
        return y.at[tok].add(yb * wt[:, None]), None

    y, _ = lax.scan(body, jnp.zeros_like(x),
                    (slot_tok.reshape(n_blocks, MOE_BLOCK), slot_w.reshape(n_blocks, MOE_BLOCK), block_e))
    return y


def moe_ffn(h, lp):
    b, t, d = h.shape
    x = h.reshape(b * t, d)
    scores = jax.nn.sigmoid((x @ lp["router_w"]).astype(jnp.float32))
    _, idx = lax.top_k(scores + lp["router_bias"].astype(jnp.float32), TOP_K)
    wts = jnp.take_along_axis(scores, idx, axis=-1)
    wts = (wts / jnp.sum(wts, axis=-1, keepdims=True) * ROUTE_SCALE).astype(x.dtype)
    y = routed_experts(x, idx, wts, lp["exp_w_gu"], lp["exp_w_down"]) + swiglu(x, lp["sh_w_gu"], lp["sh_w_down"])
    return y.reshape(b, t, d)


def modulation(cond, w_mod, b_mod):
    m = jax.nn.silu(cond) @ w_mod + b_mod
    return jnp.split(m[:, None, :], 6, axis=-1)


def mixer(h, lp, cache):
    b, t, _ = h.shape
    latent = cache is not None
    (q_c, kv_c, k_r, glu_in, g_q, g_k, g_v, d_q, d_k, d_v, d_z, d_a, d_b, gate_logits) = split_cols(h @ lp["w_in"])

    q_a = (rms_norm(q_c, lp["mla_q_norm"]) @ lp["mla_w_q_up"]).reshape(b, t, MLA_HEADS, MLA_QK)
    kv_lat = rms_norm(kv_c, lp["mla_kv_norm"])
    if latent:
        q_a = jnp.concatenate([q_a[..., :MLA_NOPE], axial_rope(q_a[..., MLA_NOPE:])], axis=-1)
        kv_all = jnp.concatenate([cache[0], kv_lat], axis=1)
        kr_all = jnp.concatenate([cache[1][:, :, None, :], axial_rope(k_r[:, :, None, :])], axis=1)
    else:
        kv_all, kr_all = kv_lat, k_r[:, :, None, :]
    kv_up = (kv_all @ lp["mla_w_kv_up"]).reshape(b, -1, MLA_HEADS, MLA_NOPE + MLA_V)
    k_a = jnp.concatenate([kv_up[..., :MLA_NOPE],
                           jnp.broadcast_to(kr_all, kv_up.shape[:3] + (MLA_ROPE,))], axis=-1)
    o_a = blocked_attention(q_a, k_a, kv_up[..., MLA_NOPE:], MLA_QK ** -0.5)

    o_b = conformer_conv(glu_in, lp["conv_dw_w"], lp["conv_dw_b"], lp["conv_ln_w"], lp["conv_ln_b"])

    q_g = rms_norm(g_q.reshape(b, t, GQA_HEADS, GQA_HD), lp["gqa_q_norm"])
    k_g = rms_norm(g_k.reshape(b, t, GQA_KV_HEADS, GQA_HD), lp["gqa_k_norm"])
    v_g = g_v.reshape(b, t, GQA_KV_HEADS, GQA_HD)
    if latent:
        q_g = axial_rope(q_g)
        k_all = jnp.concatenate([cache[2], axial_rope(k_g)], axis=1)
        v_all = jnp.concatenate([cache[3], v_g], axis=1)
    else:
        k_all, v_all = k_g, v_g
    o_c = blocked_attention(q_g, k_all, v_all, GQA_HD ** -0.5)

    o_d, s_dn = deltanet_branch(d_q, d_k, d_v, d_z, d_a, d_b, lp["dn_conv_w"], lp["dn_a_log"],
                                lp["dn_dt_bias"], lp["dn_o_norm"], cache[4] if latent else None)

    gates = jax.nn.sigmoid(gate_logits.reshape(b, t, N_BRANCH, D_MODEL))
    branches = (o_a.reshape(b, t, BRANCH_W), o_b, o_c.reshape(b, t, BRANCH_W), o_d)
    terms = [gates[:, :, i] * (branches[i] @ lp["w_branch"][i]) for i in range(N_BRANCH)]
    out = (terms[0] + terms[1] + terms[2] + terms[3]) @ lp["w_out"]
    ctx = None if latent else (kv_lat, k_r, k_g, v_g, s_dn)
    return out, ctx


def trunk_layer(x, cond, lp, cache):
    sh_m, sc_m, g_m, sh_f, sc_f, g_f = modulation(cond, lp["w_mod"], lp["b_mod"])
    h = rms_norm(x, lp["norm_mix"]) * (1.0 + sc_m) + sh_m
    mix, ctx = mixer(h, lp, cache)
    x = x + g_m * mix
    h = rms_norm(x, lp["norm_ffn"]) * (1.0 + sc_f) + sh_f
    x = x + g_f * moe_ffn(h, lp)
    return x, ctx


def setup_inputs(seed: int = 0) -> dict:
    key = jax.random.key(seed)
    keys = iter(jax.random.split(key, 48))
    f32 = jnp.float32
    L, D = DEPTH, D_MODEL

    def normal(shape, scale):
        return jax.random.normal(next(keys), shape, f32) * scale

    def gain(shape):
        return 1.0 + 0.05 * jax.random.normal(next(keys), shape, f32)

    x_prompt = normal((BATCH, SEQ, D), 1.0)
    x_sample = normal((DEC_BATCH, DEC_SEQ, D), 1.0)
    cache_mla_kv = normal((DEC_BATCH, L, PAST_LEN, KV_LORA), 1.0)
    cache_mla_kr = normal((DEC_BATCH, L, PAST_LEN, MLA_ROPE), 1.0)
    cache_gqa_k = normal((DEC_BATCH, L, PAST_LEN, GQA_KV_HEADS, GQA_HD), 1.0)
    cache_gqa_v = normal((DEC_BATCH, L, PAST_LEN, GQA_KV_HEADS, GQA_HD), 1.0)
    state_dn = normal((DEC_BATCH, L, 2, DN_HEADS, DN_DK, DN_DV), DN_DK ** -0.5)
    c = normal((DEC_BATCH, D), 1.0)
    c_ctx = normal((D,), 1.0)
    w_mod = normal((L, D, 6 * D), 0.5 * D ** -0.5)
    b_mod = normal((L, 6 * D), 0.01)
    norm_mix = gain((L, D))
    norm_ffn = gain((L, D))
    w_in = normal((L, D, sum(IN_SIZES)), D ** -0.5)
    mla_q_norm = gain((L, Q_LORA))
    mla_w_q_up = normal((L, Q_LORA, MLA_HEADS * MLA_QK), Q_LORA ** -0.5)
    mla_kv_norm = gain((L, KV_LORA))
    mla_w_kv_up = normal((L, KV_LORA, MLA_HEADS * (MLA_NOPE + MLA_V)), KV_LORA ** -0.5)
    conv_dw_w = normal((L, CONV_K, CONV_CH), CONV_K ** -0.5)
    conv_dw_b = normal((L, CONV_CH), 0.01)
    conv_ln_w = gain((L, CONV_CH))
    conv_ln_b = normal((L, CONV_CH), 0.01)
    gqa_q_norm = gain((L, GQA_HD))
    gqa_k_norm = gain((L, GQA_HD))
    dn_conv_w = normal((L, DN_CONV, DN_HEADS * (2 * DN_DK + DN_DV)), DN_CONV ** -0.5)
    dn_a_log = jnp.log(jax.random.uniform(next(keys), (L, 2, DN_HEADS), f32, 1.0, 16.0))
    dt = jnp.exp(jax.random.uniform(next(keys), (L, 2, DN_HEADS), f32, math.log(1e-3), math.log(1e-1)))
    dn_dt_bias = dt + jnp.log(-jnp.expm1(-dt))
    dn_o_norm = gain((L, DN_DV))
    w_branch = normal((L, N_BRANCH, BRANCH_W, D), BRANCH_W ** -0.5)
    w_out = normal((L, D, D), D ** -0.5)
    router_w = normal((L, D, N_EXPERTS), D ** -0.5)
    router_bias = normal((L, N_EXPERTS), 0.01)
    exp_w_gu = normal((L, N_EXPERTS, D, 2 * EXPERT_F), D ** -0.5)
    exp_w_down = normal((L, N_EXPERTS, EXPERT_F, D), EXPERT_F ** -0.5)
    sh_w_gu = normal((L, D, 2 * SHARED_F), D ** -0.5)
    sh_w_down = normal((L, SHARED_F, D), SHARED_F ** -0.5)
    final_norm = gain((D,))
    return {"x_prompt": x_prompt, "x_sample": x_sample, "cache_mla_kv": cache_mla_kv,
            "cache_mla_kr": cache_mla_kr, "cache_gqa_k": cache_gqa_k, "cache_gqa_v": cache_gqa_v,
            "state_dn": state_dn, "c": c, "c_ctx": c_ctx, "w_mod": w_mod, "b_mod": b_mod,
            "norm_mix": norm_mix, "norm_ffn": norm_ffn, "w_in": w_in, "mla_q_norm": mla_q_norm,
            "mla_w_q_up": mla_w_q_up, "mla_kv_norm": mla_kv_norm, "mla_w_kv_up": mla_w_kv_up,
            "conv_dw_w": conv_dw_w, "conv_dw_b": conv_dw_b, "conv_ln_w": conv_ln_w, "conv_ln_b": conv_ln_b,
            "gqa_q_norm": gqa_q_norm, "gqa_k_norm": gqa_k_norm, "dn_conv_w": dn_conv_w,
            "dn_a_log": dn_a_log, "dn_dt_bias": dn_dt_bias, "dn_o_norm": dn_o_norm,
            "w_branch": w_branch, "w_out": w_out, "router_w": router_w, "router_bias": router_bias,
            "exp_w_gu": exp_w_gu, "exp_w_down": exp_w_down, "sh_w_gu": sh_w_gu, "sh_w_down": sh_w_down,
            "final_norm": final_norm}


def reference(x_prompt, x_sample, cache_mla_kv, cache_mla_kr, cache_gqa_k, cache_gqa_v, state_dn, c, c_ctx,
              w_mod, b_mod, norm_mix, norm_ffn, w_in, mla_q_norm, mla_w_q_up, mla_kv_norm, mla_w_kv_up,
              conv_dw_w, conv_dw_b, conv_ln_w, conv_ln_b, gqa_q_norm, gqa_k_norm, dn_conv_w, dn_a_log,
              dn_dt_bias, dn_o_norm, w_branch, w_out, router_w, router_bias, exp_w_gu, exp_w_down,
              sh_w_gu, sh_w_down, final_norm):
    xp, xs = x_prompt, x_sample
    ctx_cond = c_ctx[None, :]
    kv_list, kr_list, gk_list, gv_list, dn_list = [], [], [], [], []
    for l in range(DEPTH):
        lp = {"w_mod": w_mod[l], "b_mod": b_mod[l], "norm_mix": norm_mix[l], "norm_ffn": norm_ffn[l],
              "w_in": w_in[l], "mla_q_norm": mla_q_norm[l], "mla_w_q_up": mla_w_q_up[l],
              "mla_kv_norm": mla_kv_norm[l], "mla_w_kv_up": mla_w_kv_up[l], "conv_dw_w": conv_dw_w[l],
              "conv_dw_b": conv_dw_b[l], "conv_ln_w": conv_ln_w[l], "conv_ln_b": conv_ln_b[l],
              "gqa_q_norm": gqa_q_norm[l], "gqa_k_norm": gqa_k_norm[l], "dn_conv_w": dn_conv_w[l],
              "dn_a_log": dn_a_log[l], "dn_dt_bias": dn_dt_bias[l], "dn_o_norm": dn_o_norm[l],
              "w_branch": w_branch[l], "w_out": w_out[l], "router_w": router_w[l],
              "router_bias": router_bias[l], "exp_w_gu": exp_w_gu[l], "exp_w_down": exp_w_down[l],
              "sh_w_gu": sh_w_gu[l], "sh_w_down": sh_w_down[l]}
        xp, (kv_l, kr_l, gk_l, gv_l, dn_l) = trunk_layer(xp, ctx_cond, lp, None)
        kv_list.append(kv_l)
        kr_list.append(kr_l)
        gk_list.append(gk_l)
        gv_list.append(gv_l)
        dn_list.append(dn_l)
        cache_l = (cache_mla_kv[:, l], cache_mla_kr[:, l], cache_gqa_k[:, l], cache_gqa_v[:, l], state_dn[:, l])
        xs, _ = trunk_layer(xs, c, lp, cache_l)
    y_prompt = rms_norm(xp, final_norm)
    y_sample = rms_norm(xs, final_norm)
    new_mla_kv = jnp.stack(kv_list, axis=1)
    new_mla_kr = jnp.stack(kr_list, axis=1)
    new_gqa_k = jnp.stack(gk_list, axis=1)
    new_gqa_v = jnp.stack(gv_list, axis=1)
    new_state_dn = jnp.stack(dn_list, axis=1)
    return (y_prompt, y_sample, new_mla_kv, new_mla_kr, new_gqa_k, new_gqa_v, new_state_dn)
```

```python
import functools
import math

import numpy as np
import jax
import jax.numpy as jnp
from jax import lax
from jax.experimental import pallas as pl
from jax.experimental.pallas import tpu as pltpu

F32 = jnp.float32
BF16 = jnp.bfloat16
EPS = 1e-6
NEG_INF = float("-inf")

GRID_W = 64
ROPE_THETA = 10000.0
MLA_HEADS, MLA_NOPE, MLA_ROPE, MLA_V = 8, 128, 64, 128
MLA_QK = MLA_NOPE + MLA_ROPE
Q_LORA, KV_LORA = 768, 512
CONV_CH, CONV_K = 1024, 31
GQA_HEADS, GQA_KV_HEADS, GQA_HD = 8, 2, 128
DN_HEADS, DN_DK, DN_DV, DN_CONV, DN_CHUNK = 8, 128, 128, 5, 64
N_BRANCH, BRANCH_W = 4, 1024
TOP_K = 8
ROUTE_SCALE = 2.5
LANES = 128
SEQ_TILE = 256
EXPERT_BLOCK = 256
HEAD_W = 2 * LANES


def _params(sem, vmem_mb):
    return pltpu.CompilerParams(dimension_semantics=sem, vmem_limit_bytes=vmem_mb << 20)


def _pick(n, pref):
    t = min(pref, n)
    while n % t:
        t //= 2
    return t


def _dot(a, b, precision=None):
    return jnp.dot(a, b, preferred_element_type=F32, precision=precision)


def _dot_nt(a, b):
    return lax.dot_general(a, b, (((1,), (1,)), ((), ())), preferred_element_type=F32)


def _dot_tn(a, b):
    return lax.dot_general(a, b, (((0,), (0,)), ((), ())), preferred_element_type=F32)


def _silu(x):
    return x * jax.nn.sigmoid(x)


def _rms(x, gain):
    return x * lax.rsqrt(jnp.mean(x * x, axis=-1, keepdims=True) + EPS) * gain


def _rope(x, cos, sin, quarter):
    n = x.shape[-1]
    lane = lax.broadcasted_iota(jnp.int32, x.shape, x.ndim - 1)
    first = (lane % (2 * quarter)) < quarter
    sw = jnp.where(first, pltpu.roll(x, n - quarter, x.ndim - 1), pltpu.roll(x, quarter, x.ndim - 1))
    return x * cos + sw * sin


def _mod_kernel(c_ref, w_ref, b_ref, o_ref):
    a = _silu(c_ref[...]).astype(BF16)
    o_ref[...] = _dot(a, w_ref[...].astype(BF16)) + b_ref[...]


def _modulation(cond8, w_mod, b_mod, l):
    d, n = w_mod.shape[1], w_mod.shape[2]
    tn = _pick(n, 512)
    return pl.pallas_call(
        _mod_kernel,
        out_shape=jax.ShapeDtypeStruct((8, n), F32),
        grid=(n // tn,),
        in_specs=[pl.BlockSpec((8, d), lambda j: (0, 0)),
                  pl.BlockSpec((None, d, tn), lambda j: (l, 0, j)),
                  pl.BlockSpec((None, 1, tn), lambda j: (l, 0, j))],
        out_specs=pl.BlockSpec((8, tn), lambda j: (0, j)),
        compiler_params=_params(("arbitrary",), 40),
        name="modulation",
    )(cond8, w_mod, b_mod.reshape(b_mod.shape[0], 1, n))


def _norm_mod_kernel(x_ref, g_ref, sc_ref, sh_ref, o_ref):
    y = _rms(x_ref[...], g_ref[...])
    o_ref[...] = (y * (1.0 + sc_ref[...]) + sh_ref[...]).astype(o_ref.dtype)


def _norm_mod(x, gain, l, mod4, k_scale, k_shift, row_of):
    t, d = x.shape
    tm = SEQ_TILE
    return pl.pallas_call(
        _norm_mod_kernel,
        out_shape=jax.ShapeDtypeStruct((t, d), BF16),
        grid=(t // tm,),
        in_specs=[pl.BlockSpec((tm, d), lambda i: (i, 0)),
                  pl.BlockSpec((None, 1, d), lambda i: (l, 0, 0)),
                  pl.BlockSpec((None, None, 1, d), lambda i: (row_of(i), k_scale, 0, 0)),
                  pl.BlockSpec((None, None, 1, d), lambda i: (row_of(i), k_shift, 0, 0))],
        out_specs=pl.BlockSpec((tm, d), lambda i: (i, 0)),
        compiler_params=_params(("parallel",), 32),
        name="norm_mod",
    )(x, gain.reshape(gain.shape[0], 1, d), mod4, mod4)


def _final_norm_kernel(x_ref, g_ref, o_ref):
    o_ref[...] = _rms(x_ref[...], g_ref[...])


def _final_norm(x, gain):
    t, d = x.shape
    tm = SEQ_TILE
    return pl.pallas_call(
        _final_norm_kernel,
        out_shape=jax.ShapeDtypeStruct((t, d), F32),
        grid=(t // tm,),
        in_specs=[pl.BlockSpec((tm, d), lambda i: (i, 0)),
                  pl.BlockSpec((1, d), lambda i: (0, 0))],
        out_specs=pl.BlockSpec((tm, d), lambda i: (i, 0)),
        compiler_params=_params(("parallel",), 32),
        name="final_norm",
    )(x, gain.reshape(1, d))


def _mm_kernel(a_ref, w_ref, o_ref):
    o_ref[...] = _dot(a_ref[...], w_ref[...]).astype(o_ref.dtype)


def _matmul(a, w, out_dtype, name, tm_pref=1024, tn_pref=512):
    m, k = a.shape
    n = w.shape[1]
    tm, tn = _pick(m, tm_pref), _pick(n, tn_pref)
    return pl.pallas_call(
        _mm_kernel,
        out_shape=jax.ShapeDtypeStruct((m, n), out_dtype),
        grid=(n // tn, m // tm),
        in_specs=[pl.BlockSpec((tm, k), lambda j, i: (i, 0)),
                  pl.BlockSpec((k, tn), lambda j, i: (0, j))],
        out_specs=pl.BlockSpec((tm, tn), lambda j, i: (i, j)),
        compiler_params=_params(("parallel", "parallel"), 48),
        name=name,
    )(a, w)


def _mm_res_kernel(a_ref, w_ref, res_ref, gate_ref, *rest):
    o_ref = rest[-1]
    y = _dot(a_ref[...], w_ref[...])
    if len(rest) == 2:
        y = y + rest[0][...]
    o_ref[...] = res_ref[...] + gate_ref[...] * y


def _matmul_residual(a, w, res, mod4, k_gate, row_of, add, name):
    m, k = a.shape
    n = w.shape[1]
    tm, tn = SEQ_TILE * 2, _pick(n, 512)
    row2 = lambda i: row_of(2 * i)
    in_specs = [pl.BlockSpec((tm, k), lambda j, i: (i, 0)),
                pl.BlockSpec((k, tn), lambda j, i: (0, j)),
                pl.BlockSpec((tm, tn), lambda j, i: (i, j)),
                pl.BlockSpec((None, None, 1, tn), lambda j, i: (row2(i), k_gate, 0, j))]
    args = [a, w, res, mod4]
    if add is not None:
        in_specs.append(pl.BlockSpec((tm, tn), lambda j, i: (i, j)))
        args.append(add)
    return pl.pallas_call(
        _mm_res_kernel,
        out_shape=jax.ShapeDtypeStruct((m, n), F32),
        grid=(n // tn, m // tm),
        in_specs=in_specs,
        out_specs=pl.BlockSpec((tm, tn), lambda j, i: (i, j)),
        compiler_params=_params(("parallel", "parallel"), 48),
        name=name,
    )(*args)


def _mla_q_kernel(qc_ref, g_ref, w_ref, cos_ref, sin_ref, o_ref):
    xn = _rms(qc_ref[...], g_ref[...]).astype(BF16)
    q = _dot(xn, w_ref[...])
    cos, sin = cos_ref[...], sin_ref[...]
    for h in range(MLA_HEADS):
        qh = q[:, h * HEAD_W:(h + 1) * HEAD_W]
        o_ref[:, h * HEAD_W:(h + 1) * HEAD_W] = _rope(qh, cos, sin, MLA_ROPE // 4).astype(BF16)


def _mla_kv_kernel(kvc_ref, g_ref, w_ref, kr_ref, cos_ref, sin_ref, lat_ref, up_ref, kro_ref):
    lat = _rms(kvc_ref[...], g_ref[...])
    lat_ref[...] = lat
    up_ref[...] = _dot(lat.astype(BF16), w_ref[...]).astype(BF16)
    kro_ref[...] = _rope(kr_ref[...], cos_ref[...], sin_ref[...], MLA_ROPE // 4).astype(BF16)


def _softmax_pv(s_list, v_list):
    m = functools.reduce(jnp.maximum, [jnp.max(s, axis=-1, keepdims=True) for s in s_list])
    p_list = [jnp.exp(s - m) for s in s_list]
    den = functools.reduce(jnp.add, [jnp.sum(p, axis=-1, keepdims=True) for p in p_list])
    acc = functools.reduce(jnp.add, [_dot(p.astype(BF16), v) for p, v in zip(p_list, v_list)])
    return acc / den


def _mla_attn_kernel(*refs, has_cache):
    if has_cache:
        q_ref, kv_ref, kr_ref, kvc_ref, krc_ref, o_ref = refs
    else:
        q_ref, kv_ref, kr_ref, o_ref = refs
    scale = MLA_QK ** -0.5
    for h in range(MLA_HEADS):
        lo = h * HEAD_W
        qn = q_ref[:, lo:lo + LANES]
        qr = q_ref[:, lo + LANES:lo + HEAD_W]
        parts = [(kv_ref, kr_ref)] + ([(kvc_ref, krc_ref)] if has_cache else [])
        s_list, v_list = [], []
        for kvr, krr in parts:
            s = _dot_nt(qn, kvr[:, lo:lo + LANES]) + _dot_nt(qr, krr[...])
            s_list.append(s * scale)
            v_list.append(kvr[:, lo + LANES:lo + HEAD_W])
        o_ref[:, h * MLA_V:(h + 1) * MLA_V] = _softmax_pv(s_list, v_list).astype(BF16)


def _gqa_attn_kernel(*refs, has_cache):
    if has_cache:
        q_ref, k_ref, v_ref, kc_ref, vc_ref, o_ref = refs
    else:
        q_ref, k_ref, v_ref, o_ref = refs
    scale = GQA_HD ** -0.5
    grp = GQA_HEADS // GQA_KV_HEADS
    for h in range(GQA_HEADS):
        lo = (h // grp) * GQA_HD
        q = q_ref[:, h * GQA_HD:(h + 1) * GQA_HD]
        parts = [(k_ref, v_ref)] + ([(kc_ref, vc_ref)] if has_cache else [])
        s_list = [_dot_nt(q, kr[:, lo:lo + GQA_HD]) * scale for kr, _ in parts]
        v_list = [vr[:, lo:lo + GQA_HD] for _, vr in parts]
        o_ref[:, h * GQA_HD:(h + 1) * GQA_HD] = _softmax_pv(s_list, v_list).astype(BF16)


def _attention(kernel, q, kv_list, cache_list, out_w, n_seq, seq_len, row0, past, name):
    tq = SEQ_TILE
    qb = seq_len // tq
    blk0 = row0 // tq
    seq0 = row0 // seq_len
    has_cache = cache_list is not None
    in_specs = [pl.BlockSpec((tq, q.shape[1]), lambda b, i: (blk0 + b * qb + i, 0))]
    in_specs += [pl.BlockSpec((seq_len, a.shape[1]), lambda b, i: (seq0 + b, 0)) for a in kv_list]
    args = [q] + list(kv_list)
    if has_cache:
        in_specs += [pl.BlockSpec((past, a.shape[1]), lambda b, i: (b, 0)) for a in cache_list]
        args += list(cache_list)
    return pl.pallas_call(
        functools.partial(kernel, has_cache=has_cache),
        out_shape=jax.ShapeDtypeStruct((n_seq * seq_len, out_w), BF16),
        grid=(n_seq, qb),
        in_specs=in_specs,
        out_specs=pl.BlockSpec((tq, out_w), lambda b, i: (b * qb + i, 0)),
        compiler_params=_params(("parallel", "parallel"), 48),
        name=name,
    )(*args)


def _gqa_prep_kernel(q_ref, k_ref, v_ref, gq_ref, gk_ref, cos_ref, sin_ref, qo_ref, ko_ref, vo_ref, kn_ref):
    cos, sin = cos_ref[...], sin_ref[...]
    for h in range(GQA_HEADS):
        sl = slice(h * GQA_HD, (h + 1) * GQA_HD)
        qo_ref[:, sl] = _rope(_rms(q_ref[:, sl], gq_ref[...]), cos, sin, GQA_HD // 4).astype(BF16)
    for h in range(GQA_KV_HEADS):
        sl = slice(h * GQA_HD, (h + 1) * GQA_HD)
        kn = _rms(k_ref[:, sl], gk_ref[...])
        kn_ref[:, sl] = kn
        ko_ref[:, sl] = _rope(kn, cos, sin, GQA_HD // 4).astype(BF16)
    vo_ref[...] = v_ref[...].astype(BF16)


def _fill_halo_buffer(buf, main, prev, nxt, first, last, halo):
    tm = main.shape[0]
    buf[pl.ds(halo, tm), :] = main
    buf[pl.ds(0, halo), :] = jnp.where(first, 0.0, prev)
    buf[pl.ds(halo + tm, halo), :] = jnp.where(last, 0.0, nxt)


def _depthwise(buf, w_ref, width, halo, tm, lo, w_lo, n):
    pad = width // 2
    acc = jnp.zeros((tm, n), F32)
    for k in range(width):
        acc = acc + w_ref[k:k + 1, w_lo:w_lo + n] * buf[pl.ds(halo - pad + k, tm), lo:lo + n]
    return acc


def _glu(x):
    return x[:, :CONV_CH] * jax.nn.sigmoid(x[:, CONV_CH:])


def _conformer_kernel(edge_ref, x_ref, xp_ref, xn_ref, w_ref, b_ref, lw_ref, lb_ref, o_ref, buf, ybuf):
    i = pl.program_id(0)
    first, last = edge_ref[0, i] == 1, edge_ref[1, i] == 1
    halo = xp_ref.shape[0]
    tm = x_ref.shape[0]
    _fill_halo_buffer(buf, _glu(x_ref[...]), _glu(xp_ref[...]), _glu(xn_ref[...]), first, last, halo)
    for c in range(CONV_CH // LANES):
        lo = c * LANES
        ybuf[:, lo:lo + LANES] = _depthwise(buf, w_ref, CONV_K, halo, tm, lo, lo, LANES) + b_ref[:, lo:lo + LANES]
    y = ybuf[...]
    yc = y - jnp.mean(y, axis=-1, keepdims=True)
    yn = yc * lax.rsqrt(jnp.mean(yc * yc, axis=-1, keepdims=True) + EPS) * lw_ref[...] + lb_ref[...]
    o_ref[...] = _silu(yn).astype(BF16)


def _dn_prep_kernel(edge_ref, q_ref, k_ref, v_ref, qp_ref, kp_ref, vp_ref, qn_ref, kn_ref, vn_ref,
                    ab_ref, w_ref, alog_ref, dtb_ref, qo_ref, ko_ref, vo_ref, gb_ref, buf):
    i = pl.program_id(0)
    first, last = edge_ref[0, i] == 1, edge_ref[1, i] == 1
    halo = qp_ref.shape[0]
    tm = q_ref.shape[0]
    width = DN_HEADS * DN_DK
    groups = ((q_ref, qp_ref, qn_ref, qo_ref, DN_DK ** -0.5), (k_ref, kp_ref, kn_ref, ko_ref, 1.0),
              (v_ref, vp_ref, vn_ref, vo_ref, None))
    for s, (x_ref, xp_ref, xn_ref, o_ref, norm_scale) in enumerate(groups):
        _fill_halo_buffer(buf, x_ref[...], xp_ref[...], xn_ref[...], first, last, halo)
        for h in range(DN_HEADS):
            lo = h * DN_DK
            y = _silu(_depthwise(buf, w_ref, DN_CONV, halo, tm, lo, s * width + lo, DN_DK))
            if norm_scale is not None:
                y = y * (lax.rsqrt(jnp.sum(y * y, axis=-1, keepdims=True) + EPS) * norm_scale)
            o_ref[:, lo:lo + DN_DK] = y
    x = ab_ref[...]
    lane = lax.broadcasted_iota(jnp.int32, x.shape, 1)
    z = x + dtb_ref[...]
    softplus = jnp.maximum(z, 0.0) + jnp.log(1.0 + jnp.exp(-jnp.abs(z)))
    g = -jnp.exp(alog_ref[...]) * softplus
    gb_ref[...] = jnp.where(lane < 2 * DN_HEADS, g, jax.nn.sigmoid(x))


def _unit_triangular_inverse(lmat):
    n = lmat.shape[0]
    hi = lax.Precision.HIGHEST
    row = lax.broadcasted_iota(jnp.int32, (n, n), 0)
    col = lax.broadcasted_iota(jnp.int32, (n, n), 1)
    p = -lmat
    x = jnp.where(row == col, 1.0, 0.0) + p
    steps = int(math.log2(n)) - 1
    for _ in range(steps):
        p = _dot(p, p, hi)
        x = x + _dot(x, p, hi)
    return x


def _dn_scan_kernel(*refs, n_chunks, has_init, want_state):
    names = ["qf", "kf", "vf", "gf", "qb", "kb", "vb", "gb"]
    refs = list(refs)
    ins = {n: refs.pop(0) for n in names}
    s0_ref = refs.pop(0) if has_init else None
    of_ref, ob_ref = refs.pop(0), refs.pop(0)
    sout_ref = refs.pop(0) if want_state else None
    state = refs.pop(0)
    c = pl.program_id(1)
    hi = lax.Precision.HIGHEST
    n = DN_CHUNK

    @pl.when(c == 0)
    def _():
        if has_init:
            state[...] = s0_ref[...]
        else:
            state[...] = jnp.zeros(state.shape, F32)

    row = lax.broadcasted_iota(jnp.int32, (n, n), 0)
    col = lax.broadcasted_iota(jnp.int32, (n, n), 1)
    for d, (qn, kn, vn, gn, o_ref) in enumerate((("qf", "kf", "vf", "gf", of_ref), ("qb", "kb", "vb", "gb", ob_ref))):
        incl = (row >= col) if d == 0 else (row <= col)
        strict = (row > col) if d == 0 else (row < col)
        gb = ins[gn][...]
        gcum = _dot(incl.astype(F32), gb, hi)
        gcum_t = gcum.T
        gtot = jnp.sum(gb, axis=0, keepdims=True)
        for h in range(DN_HEADS):
            j = d * DN_HEADS + h
            sl = slice(h * DN_DK, (h + 1) * DN_DK)
            q, k, v = ins[qn][:, sl], ins[kn][:, sl], ins[vn][:, sl]
            gc = gcum[:, j:j + 1]
            beta = gb[:, 2 * DN_HEADS + j:2 * DN_HEADS + j + 1]
            decay = jnp.exp(jnp.where(incl, gc - gcum_t[j:j + 1, :], NEG_INF))
            kb = k * beta
            lmat = jnp.where(strict, _dot_nt(kb, k) * decay, 0.0)
            tinv = _unit_triangular_inverse(lmat)
            eg = jnp.exp(gc)
            sol = _dot(tinv, jnp.concatenate([v * beta, kb * eg], axis=1), hi)
            u, w = sol[:, :DN_DV], sol[:, DN_DV:]
            qk = _dot_nt(q, k) * decay
            g_last = gtot[:, j:j + 1]
            s = state[j]
            delta = u - _dot(w, s)
            o_ref[:, sl] = _dot(q * eg, s) + _dot(qk, delta)
            state[j] = s * jnp.exp(g_last) + _dot_tn(k * jnp.exp(g_last - gc), delta)

    if want_state:
        @pl.when(c == n_chunks - 1)
        def _():
            sout_ref[...] = state[...]


def _dn_scan(q, k, v, gb, s0, n_seq, seq_len, row0, want_state, name):
    n_chunks = seq_len // DN_CHUNK
    blk0 = row0 // DN_CHUNK
    w = DN_HEADS * DN_DK
    nst = 2 * DN_HEADS
    fwd = lambda b, c: (blk0 + b * n_chunks + c, 0)
    bwd = lambda b, c: (blk0 + b * n_chunks + n_chunks - 1 - c, 0)
    ofwd = lambda b, c: (b * n_chunks + c, 0)
    obwd = lambda b, c: (b * n_chunks + n_chunks - 1 - c, 0)
    in_specs, args = [], []
    for imap in (fwd, bwd):
        in_specs += [pl.BlockSpec((DN_CHUNK, w), imap)] * 3 + [pl.BlockSpec((DN_CHUNK, LANES), imap)]
        args += [q, k, v, gb]
    if s0 is not None:
        in_specs.append(pl.BlockSpec((None, nst, DN_DK, DN_DV), lambda b, c: (b, 0, 0, 0)))
        args.append(s0)
    out_shape = [jax.ShapeDtypeStruct((n_seq * seq_len, w), F32)] * 2
    out_specs = [pl.BlockSpec((DN_CHUNK, w), ofwd), pl.BlockSpec((DN_CHUNK, w), obwd)]
    if want_state:
        out_shape.append(jax.ShapeDtypeStruct((n_seq, nst, DN_DK, DN_DV), F32))
        out_specs.append(pl.BlockSpec((None, nst, DN_DK, DN_DV), lambda b, c: (b, 0, 0, 0)))
    return pl.pallas_call(
        functools.partial(_dn_scan_kernel, n_chunks=n_chunks, has_init=s0 is not None, want_state=want_state),
        out_shape=out_shape,
        grid=(n_seq, n_chunks),
        in_specs=in_specs,
        out_specs=out_specs,
        scratch_shapes=[pltpu.VMEM((nst, DN_DK, DN_DV), F32)],
        compiler_params=_params(("parallel", "arbitrary"), 32),
        name=name,
    )(*args)


def _dn_post_kernel(of_ref, ob_ref, z_ref, g_ref, o_ref):
    for h in range(DN_HEADS):
        sl = slice(h * DN_DV, (h + 1) * DN_DV)
        o = _rms(of_ref[:, sl] + ob_ref[:, sl], g_ref[...])
        o_ref[:, sl] = (o * _silu(z_ref[:, sl])).astype(BF16)


def _merge_kernel(*refs):
    b_refs, w_refs, g_refs, o_ref = refs[0:4], refs[4:8], refs[8:12], refs[12]
    acc = None
    for b_ref, w_ref, g_ref in zip(b_refs, w_refs, g_refs):
        term = jax.nn.sigmoid(g_ref[...]) * _dot(b_ref[...], w_ref[...])
        acc = term if acc is None else acc + term
    o_ref[...] = acc.astype(BF16)


def _ffn_norm_router_kernel(x_ref, g_ref, sc_ref, sh_ref, rw_ref, rb_ref,
                            h_ref, idx_ref, rank_ref, wts_ref, cnt_ref, carry, *, n_experts):
    i = pl.program_id(0)

    @pl.when(i == 0)
    def _():
        carry[...] = jnp.zeros(carry.shape, F32)

    h = _rms(x_ref[...], g_ref[...]) * (1.0 + sc_ref[...]) + sh_ref[...]
    h_ref[...] = h.astype(BF16)
    tm = h.shape[0]
    scores = jax.nn.sigmoid(_dot(h, rw_ref[...], lax.Precision.HIGHEST))
    lane = lax.broadcasted_iota(jnp.int32, scores.shape, 1)
    biased = jnp.where(lane < n_experts, scores + rb_ref[...], NEG_INF)
    chosen = []
    for _ in range(TOP_K):
        m = jnp.max(biased, axis=-1, keepdims=True)
        pick = jnp.min(jnp.where(biased == m, lane, LANES), axis=-1, keepdims=True)
        sel = lane == pick
        chosen.append((pick, sel))
        biased = jnp.where(sel, NEG_INF, biased)
    mask = functools.reduce(jnp.logical_or, [sel for _, sel in chosen])
    maskf = jnp.where(mask, 1.0, 0.0)
    picked = scores * maskf
    dense_w = picked / jnp.sum(picked, axis=-1, keepdims=True) * ROUTE_SCALE
    row = lax.broadcasted_iota(jnp.int32, (tm, tm), 0)
    col = lax.broadcasted_iota(jnp.int32, (tm, tm), 1)
    before = jnp.where(row > col, 1.0, 0.0).astype(BF16)
    rank = _dot(before, maskf.astype(BF16)) + carry[0:1, :]
    carry[...] = carry[...] + jnp.sum(maskf, axis=0, keepdims=True)
    idx8 = jnp.zeros(scores.shape, jnp.int32)
    rank8 = jnp.zeros(scores.shape, F32)
    wts8 = jnp.zeros(scores.shape, F32)
    for k, (pick, sel) in enumerate(chosen):
        here = lane == k
        idx8 = jnp.where(here, pick, idx8)
        rank8 = jnp.where(here, jnp.sum(jnp.where(sel, rank, 0.0), axis=-1, keepdims=True), rank8)
        wts8 = jnp.where(here, jnp.sum(jnp.where(sel, dense_w, 0.0), axis=-1, keepdims=True), wts8)
    idx_ref[...] = idx8
    rank_ref[...] = rank8.astype(jnp.int32)
    wts_ref[...] = wts8
    cnt_ref[...] = carry[...]


def _dispatch_kernel(slot_hbm, x_ref, xs_in, xs_hbm, slots, sem_s, sem):
    del xs_in
    i = pl.program_id(0)
    tm = x_ref.shape[0]
    cp = pltpu.make_async_copy(slot_hbm.at[i], slots, sem_s)
    cp.start()
    cp.wait()

    def issue(r, carry):
        for j in range(TOP_K):
            pltpu.make_async_copy(x_ref.at[r], xs_hbm.at[slots[r * TOP_K + j]], sem).start()
        return carry

    lax.fori_loop(0, tm, issue, 0)

    def drain(r, carry):
        for j in range(TOP_K):
            pltpu.make_async_copy(x_ref.at[r], xs_hbm.at[slots[r * TOP_K + j]], sem).wait()
        return carry

    lax.fori_loop(0, tm, drain, 0)


def _expert_kernel(be_ref, used_ref, x_ref, wgu_ref, wd_ref, o_ref):
    b = pl.program_id(0)
    f = wd_ref.shape[0]

    @pl.when(used_ref[b] == 1)
    def _():
        gu = _dot(x_ref[...], wgu_ref[...])
        act = (_silu(gu[:, :f]) * gu[:, f:]).astype(BF16)
        o_ref[...] = _dot(act, wd_ref[...]).astype(BF16)

    @pl.when(used_ref[b] == 0)
    def _():
        o_ref[...] = jnp.zeros(o_ref.shape, BF16)


def _combine_kernel(slot_hbm, wts_hbm, ys_hbm, o_ref, slots, wts, buf, sem_s, sem_w, sem):
    i = pl.program_id(0)
    tm = o_ref.shape[0]
    cs = pltpu.make_async_copy(slot_hbm.at[i], slots, sem_s)
    cw = pltpu.make_async_copy(wts_hbm.at[i], wts, sem_w)
    cs.start()
    cw.start()
    cs.wait()
    cw.wait()

    def issue(r, carry):
        for j in range(TOP_K):
            pltpu.make_async_copy(ys_hbm.at[slots[r * TOP_K + j]], buf.at[j, r], sem).start()
        return carry

    lax.fori_loop(0, tm, issue, 0)

    def drain(r, carry):
        for j in range(TOP_K):
            pltpu.make_async_copy(ys_hbm.at[slots[r * TOP_K + j]], buf.at[j, r], sem).wait()
        return carry

    lax.fori_loop(0, tm, drain, 0)

    def reduce_row(r, carry):
        acc = wts[r * TOP_K] * buf[0, r].astype(F32)
        for j in range(1, TOP_K):
            acc = acc + wts[r * TOP_K + j] * buf[j, r].astype(F32)
        o_ref[r] = acc
        return carry

    lax.fori_loop(0, tm, reduce_row, 0)


def _shared_up_kernel(h_ref, w_ref, o_ref):
    gu = _dot(h_ref[...], w_ref[...])
    f = gu.shape[1] // 2
    o_ref[...] = (_silu(gu[:, :f]) * gu[:, f:]).astype(BF16)


def _rope_tables(n_tok, n_rot, lo, width, n_identity):
    rows = n_tok // GRID_W
    row = jnp.repeat(jnp.arange(rows, dtype=F32), GRID_W)
    col = jnp.tile(jnp.arange(GRID_W, dtype=F32), rows)
    half = n_rot // 2
    inv_freq = ROPE_THETA ** (-jnp.arange(0, half, 2, dtype=F32) / half)
    ar, ac = row[:, None] * inv_freq[None, :], col[:, None] * inv_freq[None, :]
    cos = jnp.concatenate([jnp.cos(ar), jnp.cos(ar), jnp.cos(ac), jnp.cos(ac)], axis=1)
    sin = jnp.concatenate([-jnp.sin(ar), jnp.sin(ar), -jnp.sin(ac), jnp.sin(ac)], axis=1)
    cos = jnp.pad(cos, ((0, n_identity), (lo, width - lo - n_rot)), constant_values=1.0)
    cos = cos.at[n_tok:, :].set(1.0)
    sin = jnp.pad(sin, ((0, n_identity), (lo, width - lo - n_rot)))
    return cos, sin


def kernel(x_prompt, x_sample, cache_mla_kv, cache_mla_kr, cache_gqa_k, cache_gqa_v, state_dn, c, c_ctx,
           w_mod, b_mod, norm_mix, norm_ffn, w_in, mla_q_norm, mla_w_q_up, mla_kv_norm, mla_w_kv_up,
           conv_dw_w, conv_dw_b, conv_ln_w, conv_ln_b, gqa_q_norm, gqa_k_norm, dn_conv_w, dn_a_log,
           dn_dt_bias, dn_o_norm, w_branch, w_out, router_w, router_bias, exp_w_gu, exp_w_down,
           sh_w_gu, sh_w_down, final_norm):
    nb, seq, d = x_prompt.shape
    db, dseq, _ = x_sample.shape
    depth = w_mod.shape[0]
    past = cache_mla_kv.shape[2]
    n_exp, _, f2 = exp_w_gu.shape[1:]
    fexp = f2 // 2
    tp, ts = nb * seq, db * dseq
    t = tp + ts
    tm = SEQ_TILE
    assert seq == tm and past == tm and db + 1 <= 8 and tp % (2 * tm) == 0 and dseq % (2 * tm) == 0 and tp % dseq == 0
    npt, spt = tp // tm, dseq // tm
    n_tiles = t // tm

    def row_of(i):
        return jnp.where(i < npt, 0, 1 + (i - npt) // spt)

    def pos_of(i):
        return jnp.where(i < npt, spt, (i - npt) % spt)

    tile_ids = np.arange(n_tiles)
    in_sample = tile_ids >= npt
    edges = jnp.asarray(np.stack([np.where(in_sample, (tile_ids - npt) % spt == 0, True),
                                  np.where(in_sample, (tile_ids - npt) % spt == spt - 1, True)]).astype(np.int32))

    x = jnp.concatenate([x_prompt.reshape(tp, d), x_sample.reshape(ts, d)], axis=0)
    cond8 = jnp.zeros((8, d), F32).at[0].set(c_ctx).at[1:1 + db].set(c)

    cos_m, sin_m = _rope_tables(dseq, MLA_ROPE, MLA_NOPE, HEAD_W, tm)
    cos_k, sin_k = _rope_tables(dseq, MLA_ROPE, 0, LANES, tm)
    cos_g, sin_g = _rope_tables(dseq, GQA_HD, 0, GQA_HD, tm)

    sizes = (Q_LORA, KV_LORA, MLA_ROPE, 2 * CONV_CH, GQA_HEADS * GQA_HD, GQA_KV_HEADS * GQA_HD,
             GQA_KV_HEADS * GQA_HD, DN_HEADS * DN_DK, DN_HEADS * DN_DK, DN_HEADS * DN_DV, DN_HEADS * DN_DV,
             2 * DN_HEADS, 2 * DN_HEADS, N_BRANCH * d)
    names = ("q_c", "kv_c", "k_r", "glu", "g_q", "g_k", "g_v", "d_q", "d_k", "d_v", "d_z", "d_a", "d_b", "gates")
    src = dict(zip(names, zip(np.cumsum((0,) + sizes[:-1]).tolist(), sizes)))
    order = ("gates", "glu", "g_q", "d_q", "d_k", "d_v", "d_z", "kv_c", "g_k", "g_v", "q_c", "k_r", "d_ab")
    col = {}
    off = 0
    src["d_ab"] = (src["d_a"][0], 4 * DN_HEADS)
    for name in order:
        width = max(LANES, src[name][1])
        off = -(-off // width) * width
        col[name] = (off, width)
        off += width
    p_width = -(-off // 512) * 512

    def arrange_w_in(w):
        parts, pos = [], 0
        for name in order:
            a, n = src[name]
            parts.append(jnp.pad(w[:, a:a + n], ((0, 0), (col[name][0] - pos, col[name][1] - n))))
            pos = col[name][0] + col[name][1]
        parts.append(jnp.zeros((w.shape[0], p_width - pos), w.dtype))
        return jnp.concatenate(parts, axis=1).astype(BF16)

    def head_pad(w, n_in, widths, pads):
        h = w.shape[1] // sum(widths)
        w3 = w.reshape(n_in, h, sum(widths))
        parts, a = [], 0
        for wd, pd in zip(widths, pads):
            parts.append(jnp.pad(w3[:, :, a:a + wd], ((0, 0), (0, 0), (0, pd))))
            a += wd
        return jnp.concatenate(parts, axis=2).reshape(n_in, -1).astype(BF16)

    def pcol(name, width=None):
        a, n = col[name]
        width = n if width is None else width
        assert a % width == 0
        return a // width

    kv_lats, krs, gks, gvs, states = [], [], [], [], []
    for l in range(depth):
        mod4 = _modulation(cond8, w_mod, b_mod, l).reshape(8, 6, 1, d)
        h = _norm_mod(x, norm_mix, l, mod4, 1, 0, row_of)
        proj = _matmul(h, arrange_w_in(w_in[l]), F32, "in_proj")

        wq = head_pad(mla_w_q_up[l], Q_LORA, (MLA_NOPE, MLA_ROPE), (0, HEAD_W - MLA_QK))
        q_mla = pl.pallas_call(
            _mla_q_kernel,
            out_shape=jax.ShapeDtypeStruct((t, MLA_HEADS * HEAD_W), BF16),
            grid=(n_tiles,),
            in_specs=[pl.BlockSpec((tm, Q_LORA), lambda i: (i, pcol("q_c"))),
                      pl.BlockSpec((None, 1, Q_LORA), lambda i: (l, 0, 0)),
                      pl.BlockSpec((Q_LORA, MLA_HEADS * HEAD_W), lambda i: (0, 0)),
                      pl.BlockSpec((tm, HEAD_W), lambda i: (pos_of(i), 0)),
                      pl.BlockSpec((tm, HEAD_W), lambda i: (pos_of(i), 0))],
            out_specs=pl.BlockSpec((tm, MLA_HEADS * HEAD_W), lambda i: (i, 0)),
            compiler_params=_params(("parallel",), 32),
            name="mla_q",
        )(proj, mla_q_norm.reshape(depth, 1, Q_LORA), wq, cos_m, sin_m)
        wkv = mla_w_kv_up[l].astype(BF16)
        kv_lat, kv_up, kr_rot = pl.pallas_call(
            _mla_kv_kernel,
            out_shape=[jax.ShapeDtypeStruct((t, KV_LORA), F32),
                       jax.ShapeDtypeStruct((t, MLA_HEADS * HEAD_W), BF16),
                       jax.ShapeDtypeStruct((t, LANES), BF16)],
            grid=(n_tiles,),
            in_specs=[pl.BlockSpec((tm, KV_LORA), lambda i: (i, pcol("kv_c"))),
                      pl.BlockSpec((None, 1, KV_LORA), lambda i: (l, 0, 0)),
                      pl.BlockSpec((KV_LORA, MLA_HEADS * HEAD_W), lambda i: (0, 0)),
                      pl.BlockSpec((tm, LANES), lambda i: (i, pcol("k_r"))),
                      pl.BlockSpec((tm, LANES), lambda i: (pos_of(i), 0)),
                      pl.BlockSpec((tm, LANES), lambda i: (pos_of(i), 0))],
            out_specs=[pl.BlockSpec((tm, KV_LORA), lambda i: (i, 0)),
                       pl.BlockSpec((tm, MLA_HEADS * HEAD_W), lambda i: (i, 0)),
                       pl.BlockSpec((tm, LANES), lambda i: (i, 0))],
            compiler_params=_params(("parallel",), 32),
            name="mla_kv",
        )(proj, mla_kv_norm.reshape(depth, 1, KV_LORA), wkv, proj, cos_k, sin_k)
        kv_up_c = _matmul(cache_mla_kv[:, l].reshape(db * past, KV_LORA).astype(BF16), wkv, BF16, "mla_kv_cache")
        kr_c = jnp.pad(cache_mla_kr[:, l].reshape(db * past, MLA_ROPE), ((0, 0), (0, LANES - MLA_ROPE))).astype(BF16)
        o_a = jnp.concatenate([
            _attention(_mla_attn_kernel, q_mla, (kv_up, kr_rot), None, MLA_HEADS * MLA_V, nb, seq, 0, past,
                       "mla_attn_ctx"),
            _attention(_mla_attn_kernel, q_mla, (kv_up, kr_rot), (kv_up_c, kr_c), MLA_HEADS * MLA_V, db, dseq, tp,
                       past, "mla_attn_lat")], axis=0)

        halo_b = 16
        hb = tm // halo_b
        o_b = pl.pallas_call(
            _conformer_kernel,
            out_shape=jax.ShapeDtypeStruct((t, CONV_CH), BF16),
            grid_spec=pltpu.PrefetchScalarGridSpec(
                num_scalar_prefetch=1,
                grid=(n_tiles,),
                in_specs=[pl.BlockSpec((tm, 2 * CONV_CH), lambda i, e: (i, pcol("glu"))),
                          pl.BlockSpec((halo_b, 2 * CONV_CH),
                                       lambda i, e: (jnp.maximum(i * hb - 1, 0), pcol("glu"))),
                          pl.BlockSpec((halo_b, 2 * CONV_CH),
                                       lambda i, e: (jnp.minimum((i + 1) * hb, n_tiles * hb - 1), pcol("glu"))),
                          pl.BlockSpec((None, CONV_K, CONV_CH), lambda i, e: (l, 0, 0)),
                          pl.BlockSpec((None, 1, CONV_CH), lambda i, e: (l, 0, 0)),
                          pl.BlockSpec((None, 1, CONV_CH), lambda i, e: (l, 0, 0)),
                          pl.BlockSpec((None, 1, CONV_CH), lambda i, e: (l, 0, 0))],
                out_specs=pl.BlockSpec((tm, CONV_CH), lambda i, e: (i, 0)),
                scratch_shapes=[pltpu.VMEM((tm + 2 * halo_b, CONV_CH), F32), pltpu.VMEM((tm, CONV_CH), F32)]),
            compiler_params=_params(("parallel",), 32),
            name="conformer_conv",
        )(edges, proj, proj, proj, conv_dw_w, conv_dw_b.reshape(depth, 1, CONV_CH),
          conv_ln_w.reshape(depth, 1, CONV_CH), conv_ln_b.reshape(depth, 1, CONV_CH))

        wq_g, wk_g = GQA_HEADS * GQA_HD, GQA_KV_HEADS * GQA_HD
        q_g, k_g, v_g, k_norm = pl.pallas_call(
            _gqa_prep_kernel,
            out_shape=[jax.ShapeDtypeStruct((t, wq_g), BF16), jax.ShapeDtypeStruct((t, wk_g), BF16),
                       jax.ShapeDtypeStruct((t, wk_g), BF16), jax.ShapeDtypeStruct((t, wk_g), F32)],
            grid=(n_tiles,),
            in_specs=[pl.BlockSpec((tm, wq_g), lambda i: (i, pcol("g_q"))),
                      pl.BlockSpec((tm, wk_g), lambda i: (i, pcol("g_k"))),
                      pl.BlockSpec((tm, wk_g), lambda i: (i, pcol("g_v"))),
                      pl.BlockSpec((None, 1, GQA_HD), lambda i: (l, 0, 0)),
                      pl.BlockSpec((None, 1, GQA_HD), lambda i: (l, 0, 0)),
                      pl.BlockSpec((tm, GQA_HD), lambda i: (pos_of(i), 0)),
                      pl.BlockSpec((tm, GQA_HD), lambda i: (pos_of(i), 0))],
            out_specs=[pl.BlockSpec((tm, wq_g), lambda i: (i, 0)), pl.BlockSpec((tm, wk_g), lambda i: (i, 0)),
                       pl.BlockSpec((tm, wk_g), lambda i: (i, 0)), pl.BlockSpec((tm, wk_g), lambda i: (i, 0))],
            compiler_params=_params(("parallel",), 32),
            name="gqa_prep",
        )(proj, proj, proj, gqa_q_norm.reshape(depth, 1, GQA_HD), gqa_k_norm.reshape(depth, 1, GQA_HD),
          cos_g, sin_g)
        kc_g = cache_gqa_k[:, l].reshape(db * past, wk_g).astype(BF16)
        vc_g = cache_gqa_v[:, l].reshape(db * past, wk_g).astype(BF16)
        o_c = jnp.concatenate([
            _attention(_gqa_attn_kernel, q_g, (k_g, v_g), None, wq_g, nb, seq, 0, past, "gqa_attn_ctx"),
            _attention(_gqa_attn_kernel, q_g, (k_g, v_g), (kc_g, vc_g), wq_g, db, dseq, tp, past,
                       "gqa_attn_lat")], axis=0)

        halo_d = 8
        hd = tm // halo_d
        wdn = DN_HEADS * DN_DK
        prev_map = lambda name: (lambda i, e: (jnp.maximum(i * hd - 1, 0), pcol(name)))
        next_map = lambda name: (lambda i, e: (jnp.minimum((i + 1) * hd, n_tiles * hd - 1), pcol(name)))
        main_specs = [pl.BlockSpec((tm, wdn), (lambda name: (lambda i, e: (i, pcol(name))))(nm))
                      for nm in ("d_q", "d_k", "d_v")]
        prev_specs = [pl.BlockSpec((halo_d, wdn), prev_map(nm)) for nm in ("d_q", "d_k", "d_v")]
        next_specs = [pl.BlockSpec((halo_d, wdn), next_map(nm)) for nm in ("d_q", "d_k", "d_v")]
        pad16 = lambda a: jnp.pad(a.reshape(1, 2 * DN_HEADS), ((0, 0), (0, LANES - 2 * DN_HEADS)))
        qd, kd, vd, gbeta = pl.pallas_call(
            _dn_prep_kernel,
            out_shape=[jax.ShapeDtypeStruct((t, wdn), F32)] * 3 + [jax.ShapeDtypeStruct((t, LANES), F32)],
            grid_spec=pltpu.PrefetchScalarGridSpec(
                num_scalar_prefetch=1,
                grid=(n_tiles,),
                in_specs=main_specs + prev_specs + next_specs + [
                    pl.BlockSpec((tm, LANES), lambda i, e: (i, pcol("d_ab"))),
                    pl.BlockSpec((None, DN_CONV, 3 * wdn), lambda i, e: (l, 0, 0)),
                    pl.BlockSpec((1, LANES), lambda i, e: (0, 0)),
                    pl.BlockSpec((1, LANES), lambda i, e: (0, 0))],
                out_specs=[pl.BlockSpec((tm, wdn), lambda i, e: (i, 0))] * 3
                + [pl.BlockSpec((tm, LANES), lambda i, e: (i, 0))],
                scratch_shapes=[pltpu.VMEM((tm + 2 * halo_d, wdn), F32)]),
            compiler_params=_params(("parallel",), 40),
            name="dn_prep",
        )(edges, *([proj] * 10), dn_conv_w, pad16(dn_a_log[l]), pad16(dn_dt_bias[l]))
        of_p, ob_p, st_p = _dn_scan(qd, kd, vd, gbeta, None, nb, seq, 0, True, "dn_scan_ctx")
        s0 = state_dn[:, l].reshape(db, 2 * DN_HEADS, DN_DK, DN_DV)
        of_s, ob_s = _dn_scan(qd, kd, vd, gbeta, s0, db, dseq, tp, False, "dn_scan_lat")
        o_f = jnp.concatenate([of_p, of_s], axis=0)
        o_bw = jnp.concatenate([ob_p, ob_s], axis=0)
        o_d = pl.pallas_call(
            _dn_post_kernel,
            out_shape=jax.ShapeDtypeStruct((t, wdn), BF16),
            grid=(n_tiles,),
            in_specs=[pl.BlockSpec((tm, wdn), lambda i: (i, 0)), pl.BlockSpec((tm, wdn), lambda i: (i, 0)),
                      pl.BlockSpec((tm, wdn), lambda i: (i, pcol("d_z"))),
                      pl.BlockSpec((None, 1, DN_DV), lambda i: (l, 0, 0))],
            out_specs=pl.BlockSpec((tm, wdn), lambda i: (i, 0)),
            compiler_params=_params(("parallel",), 32),
            name="dn_post",
        )(o_f, o_bw, proj, dn_o_norm.reshape(depth, 1, DN_DV))

        tmm, tnm = _pick(t, 512), _pick(d, 512)
        g0 = col["gates"][0] // tnm
        wb = w_branch[l].astype(BF16)
        branch_specs = [pl.BlockSpec((tmm, BRANCH_W), lambda i, j: (i, 0))] * N_BRANCH
        w_specs = [pl.BlockSpec((None, BRANCH_W, tnm), (lambda b: (lambda i, j: (b, 0, j)))(b))
                   for b in range(N_BRANCH)]
        gate_specs = [pl.BlockSpec((tmm, tnm), (lambda b: (lambda i, j: (i, g0 + b * (d // tnm) + j)))(b))
                      for b in range(N_BRANCH)]
        merged = pl.pallas_call(
            _merge_kernel,
            out_shape=jax.ShapeDtypeStruct((t, d), BF16),
            grid=(t // tmm, d // tnm),
            in_specs=branch_specs + w_specs + gate_specs,
            out_specs=pl.BlockSpec((tmm, tnm), lambda i, j: (i, j)),
            compiler_params=_params(("parallel", "parallel"), 48),
            name="branch_merge",
        )(o_a, o_b, o_c, o_d, wb, wb, wb, wb, proj, proj, proj, proj)
        x = _matmul_residual(merged, w_out[l].astype(BF16), x, mod4, 2, row_of, None, "out_proj")

        n_pad = LANES - n_exp
        rw = jnp.pad(router_w[l], ((0, 0), (0, n_pad)))
        rb = jnp.pad(router_bias[l].reshape(1, n_exp), ((0, 0), (0, n_pad)))
        h2, idx8, rank8, wts8, counts = pl.pallas_call(
            functools.partial(_ffn_norm_router_kernel, n_experts=n_exp),
            out_shape=[jax.ShapeDtypeStruct((t, d), BF16), jax.ShapeDtypeStruct((t, LANES), jnp.int32),
                       jax.ShapeDtypeStruct((t, LANES), jnp.int32), jax.ShapeDtypeStruct((t, LANES), F32),
                       jax.ShapeDtypeStruct((8, LANES), F32)],
            grid=(n_tiles,),
            in_specs=[pl.BlockSpec((tm, d), lambda i: (i, 0)),
                      pl.BlockSpec((None, 1, d), lambda i: (l, 0, 0)),
                      pl.BlockSpec((None, None, 1, d), lambda i: (row_of(i), 4, 0, 0)),
                      pl.BlockSpec((None, None, 1, d), lambda i: (row_of(i), 3, 0, 0)),
                      pl.BlockSpec((d, LANES), lambda i: (0, 0)),
                      pl.BlockSpec((1, LANES), lambda i: (0, 0))],
            out_specs=[pl.BlockSpec((tm, d), lambda i: (i, 0)), pl.BlockSpec((tm, LANES), lambda i: (i, 0)),
                       pl.BlockSpec((tm, LANES), lambda i: (i, 0)), pl.BlockSpec((tm, LANES), lambda i: (i, 0)),
                       pl.BlockSpec((8, LANES), lambda i: (0, 0))],
            scratch_shapes=[pltpu.VMEM((8, LANES), F32)],
            compiler_params=_params(("arbitrary",), 40),
            name="ffn_norm_router",
        )(x, norm_ffn.reshape(depth, 1, d), mod4, mod4, rw, rb)

        bm = EXPERT_BLOCK
        n_blocks = (t * TOP_K + n_exp * (bm - 1) + bm - 1) // bm
        cnt = counts[0, :n_exp].astype(jnp.int32)
        padded = (cnt + bm - 1) // bm * bm
        pad_end = jnp.cumsum(padded)
        pad_start = pad_end - padded
        slot8 = (pad_start[idx8[:, :TOP_K]] + rank8[:, :TOP_K]).astype(jnp.int32)
        blk = jnp.arange(n_blocks, dtype=jnp.int32)
        block_e = jnp.minimum(jnp.searchsorted(pad_end, blk * bm, side="right"), n_exp - 1).astype(jnp.int32)
        used = (blk * bm < pad_end[-1]).astype(jnp.int32)

        rows_w = d // LANES
        n_slots = n_blocks * bm
        h2_rows = h2.reshape(t, rows_w, LANES)
        xs = pl.pallas_call(
            _dispatch_kernel,
            out_shape=jax.ShapeDtypeStruct((n_slots, rows_w, LANES), BF16),
            grid=(n_tiles,),
            in_specs=[pl.BlockSpec(memory_space=pl.ANY),
                      pl.BlockSpec((tm, rows_w, LANES), lambda i: (i, 0, 0)),
                      pl.BlockSpec(memory_space=pl.ANY)],
            out_specs=pl.BlockSpec(memory_space=pl.ANY),
            scratch_shapes=[pltpu.SMEM((tm * TOP_K,), jnp.int32), pltpu.SemaphoreType.DMA,
                            pltpu.SemaphoreType.DMA],
            input_output_aliases={2: 0},
            compiler_params=_params(("arbitrary",), 32),
            name="moe_dispatch",
        )(slot8.reshape(n_tiles, tm * TOP_K), h2_rows, jnp.zeros((n_slots, rows_w, LANES), BF16))

        wgu = exp_w_gu[l].astype(BF16)
        wdw = exp_w_down[l].astype(BF16)
        ys = pl.pallas_call(
            _expert_kernel,
            out_shape=jax.ShapeDtypeStruct((n_slots, d), BF16),
            grid_spec=pltpu.PrefetchScalarGridSpec(
                num_scalar_prefetch=2,
                grid=(n_blocks,),
                in_specs=[pl.BlockSpec((bm, d), lambda b, be, us: (b, 0)),
                          pl.BlockSpec((None, d, f2), lambda b, be, us: (be[b], 0, 0)),
                          pl.BlockSpec((None, fexp, d), lambda b, be, us: (be[b], 0, 0))],
                out_specs=pl.BlockSpec((bm, d), lambda b, be, us: (b, 0))),
            compiler_params=_params(("arbitrary",), 48),
            name="moe_experts",
        )(block_e, used, xs.reshape(n_slots, d), wgu, wdw)

        tmc = 128
        y_routed = pl.pallas_call(
            _combine_kernel,
            out_shape=jax.ShapeDtypeStruct((t, rows_w, LANES), F32),
            grid=(t // tmc,),
            in_specs=[pl.BlockSpec(memory_space=pl.ANY), pl.BlockSpec(memory_space=pl.ANY),
                      pl.BlockSpec(memory_space=pl.ANY)],
            out_specs=pl.BlockSpec((tmc, rows_w, LANES), lambda i: (i, 0, 0)),
            scratch_shapes=[pltpu.SMEM((tmc * TOP_K,), jnp.int32), pltpu.SMEM((tmc * TOP_K,), F32),
                            pltpu.VMEM((TOP_K, tmc, rows_w, LANES), BF16),
                            pltpu.SemaphoreType.DMA, pltpu.SemaphoreType.DMA, pltpu.SemaphoreType.DMA],
            compiler_params=_params(("arbitrary",), 40),
            name="moe_combine",
        )(slot8.reshape(t // tmc, tmc * TOP_K), wts8[:, :TOP_K].reshape(t // tmc, tmc * TOP_K),
          ys.reshape(n_slots, rows_w, LANES)).reshape(t, d)

        tms = _pick(t, 512)
        act_sh = pl.pallas_call(
            _shared_up_kernel,
            out_shape=jax.ShapeDtypeStruct((t, sh_w_gu.shape[2] // 2), BF16),
            grid=(t // tms,),
            in_specs=[pl.BlockSpec((tms, d), lambda i: (i, 0)),
                      pl.BlockSpec((d, sh_w_gu.shape[2]), lambda i: (0, 0))],
            out_specs=pl.BlockSpec((tms, sh_w_gu.shape[2] // 2), lambda i: (i, 0)),
            compiler_params=_params(("parallel",), 40),
            name="shared_up",
        )(h2, sh_w_gu[l].astype(BF16))
        x = _matmul_residual(act_sh, sh_w_down[l].astype(BF16), x, mod4, 5, row_of, y_routed, "ffn_out")

        kv_lats.append(kv_lat[:tp].reshape(nb, seq, KV_LORA))
        a, n = col["k_r"][0], MLA_ROPE
        krs.append(proj[:tp, a:a + n].reshape(nb, seq, MLA_ROPE))
        gks.append(k_norm[:tp].reshape(nb, seq, GQA_KV_HEADS, GQA_HD))
        a, n = col["g_v"]
        gvs.append(proj[:tp, a:a + n].reshape(nb, seq, GQA_KV_HEADS, GQA_HD))
        states.append(st_p.reshape(nb, 2, DN_HEADS, DN_DK, DN_DV))

    y = _final_norm(x, final_norm)
    return (y[:tp].reshape(nb, seq, d), y[tp:].reshape(db, dseq, d),
            jnp.stack(kv_lats, axis=1), jnp.stack(krs, axis=1), jnp.stack(gks, axis=1),
            jnp.stack(gvs, axis=1), jnp.stack(states, axis=1))
```

```python
import functools
import math

import numpy as np
import jax
import jax.numpy as jnp
from jax import lax
from jax.experimental import pallas as pl
from jax.experimental.pallas import tpu as pltpu

F32 = jnp.float32
BF16 = jnp.bfloat16
EPS = 1e-6
NEG_INF = float("-inf")

GRID_W = 64
ROPE_THETA = 10000.0
MLA_HEADS, MLA_NOPE, MLA_ROPE, MLA_V = 8, 128, 64, 128
MLA_QK = MLA_NOPE + MLA_ROPE
Q_LORA, KV_LORA = 768, 512
CONV_CH, CONV_K = 1024, 31
GQA_HEADS, GQA_KV_HEADS, GQA_HD = 8, 2, 128
DN_HEADS, DN_DK, DN_DV, DN_CONV, DN_CHUNK = 8, 128, 128, 5, 64
N_BRANCH, BRANCH_W = 4, 1024
TOP_K = 8
ROUTE_SCALE = 2.5
LANES = 128
SEQ_TILE = 256
EXPERT_BLOCK = 256
HEAD_W = 2 * LANES


def _params(sem, vmem_mb):
    return pltpu.CompilerParams(dimension_semantics=sem, vmem_limit_bytes=vmem_mb << 20)


def _pick(n, pref):
    t = min(pref, n)
    while n % t:
        t //= 2
    return t


def _dot(a, b, precision=None):
    return jnp.dot(a, b, preferred_element_type=F32, precision=precision)


def _dot_nt(a, b):
    return lax.dot_general(a, b, (((1,), (1,)), ((), ())), preferred_element_type=F32)


def _dot_tn(a, b):
    return lax.dot_general(a, b, (((0,), (0,)), ((), ())), preferred_element_type=F32)


def _silu(x):
    return x * jax.nn.sigmoid(x)


def _rms(x, gain):
    return x * lax.rsqrt(jnp.mean(x * x, axis=-1, keepdims=True) + EPS) * gain


def _rope(x, cos, sin, quarter):
    n = x.shape[-1]
    lane = lax.broadcasted_iota(jnp.int32, x.shape, x.ndim - 1)
    first = (lane % (2 * quarter)) < quarter
    sw = jnp.where(first, pltpu.roll(x, n - quarter, x.ndim - 1), pltpu.roll(x, quarter, x.ndim - 1))
    return x * cos + sw * sin


def _mod_kernel(c_ref, w_ref, b_ref, o_ref):
    a = _silu(c_ref[...]).astype(BF16)
    o_ref[...] = _dot(a, w_ref[...].astype(BF16)) + b_ref[...]


def _modulation(cond8, w_mod, b_mod, l):
    d, n = w_mod.shape[1], w_mod.shape[2]
    tn = _pick(n, 512)
    return pl.pallas_call(
        _mod_kernel,
        out_shape=jax.ShapeDtypeStruct((8, n), F32),
        grid=(n // tn,),
        in_specs=[pl.BlockSpec((8, d), lambda j: (0, 0)),
                  pl.BlockSpec((None, d, tn), lambda j: (l, 0, j)),
                  pl.BlockSpec((None, 1, tn), lambda j: (l, 0, j))],
        out_specs=pl.BlockSpec((8, tn), lambda j: (0, j)),
        compiler_params=_params(("arbitrary",), 40),
        name="modulation",
    )(cond8, w_mod, b_mod.reshape(b_mod.shape[0], 1, n))


def _norm_mod_kernel(x_ref, g_ref, sc_ref, sh_ref, o_ref):
    y = _rms(x_ref[...], g_ref[...])
    o_ref[...] = (y * (1.0 + sc_ref[...]) + sh_ref[...]).astype(o_ref.dtype)


def _norm_mod(x, gain, l, mod4, k_scale, k_shift, row_of):
    t, d = x.shape
    tm = SEQ_TILE
    return pl.pallas_call(
        _norm_mod_kernel,
        out_shape=jax.ShapeDtypeStruct((t, d), BF16),
        grid=(t // tm,),
        in_specs=[pl.BlockSpec((tm, d), lambda i: (i, 0)),
                  pl.BlockSpec((None, 1, d), lambda i: (l, 0, 0)),
                  pl.BlockSpec((None, None, 1, d), lambda i: (row_of(i), k_scale, 0, 0)),
                  pl.BlockSpec((None, None, 1, d), lambda i: (row_of(i), k_shift, 0, 0))],
        out_specs=pl.BlockSpec((tm, d), lambda i: (i, 0)),
        compiler_params=_params(("parallel",), 32),
        name="norm_mod",
    )(x, gain.reshape(gain.shape[0], 1, d), mod4, mod4)


def _final_norm_kernel(x_ref, g_ref, o_ref):
    o_ref[...] = _rms(x_ref[...], g_ref[...])


def _final_norm(x, gain):
    t, d = x.shape
    tm = SEQ_TILE
    return pl.pallas_call(
        _final_norm_kernel,
        out_shape=jax.ShapeDtypeStruct((t, d), F32),
        grid=(t // tm,),
        in_specs=[pl.BlockSpec((tm, d), lambda i: (i, 0)),
                  pl.BlockSpec((1, d), lambda i: (0, 0))],
        out_specs=pl.BlockSpec((tm, d), lambda i: (i, 0)),
        compiler_params=_params(("parallel",), 32),
        name="final_norm",
    )(x, gain.reshape(1, d))


def _mm_kernel(a_ref, w_ref, o_ref):
    o_ref[...] = _dot(a_ref[...], w_ref[...]).astype(o_ref.dtype)


def _matmul(a, w, l, out_dtype, name, tm_pref=1024, tn_pref=512):
    m, k = a.shape
    n = w.shape[2]
    tm, tn = _pick(m, tm_pref), _pick(n, tn_pref)
    return pl.pallas_call(
        _mm_kernel,
        out_shape=jax.ShapeDtypeStruct((m, n), out_dtype),
        grid=(n // tn, m // tm),
        in_specs=[pl.BlockSpec((tm, k), lambda j, i: (i, 0)),
                  pl.BlockSpec((None, k, tn), lambda j, i: (l, 0, j))],
        out_specs=pl.BlockSpec((tm, tn), lambda j, i: (i, j)),
        compiler_params=_params(("parallel", "parallel"), 48),
        name=name,
    )(a, w)


def _mm_res_kernel(a_ref, w_ref, res_ref, gate_ref, *rest):
    o_ref = rest[-1]
    y = _dot(a_ref[...], w_ref[...])
    if len(rest) == 2:
        y = y + rest[0][...]
    o_ref[...] = res_ref[...] + gate_ref[...] * y


def _matmul_residual(a, w, l, res, mod4, k_gate, row_of, add, name):
    m, k = a.shape
    n = w.shape[2]
    tm, tn = SEQ_TILE * 2, _pick(n, 512)
    row2 = lambda i: row_of(2 * i)
    in_specs = [pl.BlockSpec((tm, k), lambda j, i: (i, 0)),
                pl.BlockSpec((None, k, tn), lambda j, i: (l, 0, j)),
                pl.BlockSpec((tm, tn), lambda j, i: (i, j)),
                pl.BlockSpec((None, None, 1, tn), lambda j, i: (row2(i), k_gate, 0, j))]
    args = [a, w, res, mod4]
    if add is not None:
        in_specs.append(pl.BlockSpec((tm, tn), lambda j, i: (i, j)))
        args.append(add)
    return pl.pallas_call(
        _mm_res_kernel,
        out_shape=jax.ShapeDtypeStruct((m, n), F32),
        grid=(n // tn, m // tm),
        in_specs=in_specs,
        out_specs=pl.BlockSpec((tm, tn), lambda j, i: (i, j)),
        compiler_params=_params(("parallel", "parallel"), 48),
        name=name,
    )(*args)


def _mla_q_kernel(qc_ref, g_ref, w_ref, cos_ref, sin_ref, o_ref):
    xn = _rms(qc_ref[...], g_ref[...]).astype(BF16)
    q = _dot(xn, w_ref[...])
    cos, sin = cos_ref[...], sin_ref[...]
    for h in range(MLA_HEADS):
        qh = q[:, h * HEAD_W:(h + 1) * HEAD_W]
        o_ref[:, h * HEAD_W:(h + 1) * HEAD_W] = _rope(qh, cos, sin, MLA_ROPE // 4).astype(BF16)


def _mla_kv_kernel(kvc_ref, g_ref, w_ref, kr_ref, cos_ref, sin_ref, lat_ref, up_ref, kro_ref):
    lat = _rms(kvc_ref[...], g_ref[...])
    lat_ref[...] = lat
    up_ref[...] = _dot(lat.astype(BF16), w_ref[...]).astype(BF16)
    kro_ref[...] = _rope(kr_ref[...], cos_ref[...], sin_ref[...], MLA_ROPE // 4).astype(BF16)


def _softmax_pv(s_list, v_list):
    m = functools.reduce(jnp.maximum, [jnp.max(s, axis=-1, keepdims=True) for s in s_list])
    p_list = [jnp.exp(s - m) for s in s_list]
    den = functools.reduce(jnp.add, [jnp.sum(p, axis=-1, keepdims=True) for p in p_list])
    acc = functools.reduce(jnp.add, [_dot(p.astype(BF16), v) for p, v in zip(p_list, v_list)])
    return acc / den


def _mla_attn_kernel(*refs, has_cache):
    if has_cache:
        q_ref, kv_ref, kr_ref, kvc_ref, krc_ref, o_ref = refs
    else:
        q_ref, kv_ref, kr_ref, o_ref = refs
    scale = MLA_QK ** -0.5
    for h in range(MLA_HEADS):
        lo = h * HEAD_W
        qn = q_ref[:, lo:lo + LANES]
        qr = q_ref[:, lo + LANES:lo + HEAD_W]
        parts = [(kv_ref, kr_ref)] + ([(kvc_ref, krc_ref)] if has_cache else [])
        s_list, v_list = [], []
        for kvr, krr in parts:
            s = _dot_nt(qn, kvr[:, lo:lo + LANES]) + _dot_nt(qr, krr[...])
            s_list.append(s * scale)
            v_list.append(kvr[:, lo + LANES:lo + HEAD_W])
        o_ref[:, h * MLA_V:(h + 1) * MLA_V] = _softmax_pv(s_list, v_list).astype(BF16)


def _gqa_attn_kernel(*refs, has_cache):
    if has_cache:
        q_ref, k_ref, v_ref, kc_ref, vc_ref, o_ref = refs
    else:
        q_ref, k_ref, v_ref, o_ref = refs
    scale = GQA_HD ** -0.5
    grp = GQA_HEADS // GQA_KV_HEADS
    for h in range(GQA_HEADS):
        lo = (h // grp) * GQA_HD
        q = q_ref[:, h * GQA_HD:(h + 1) * GQA_HD]
        parts = [(k_ref, v_ref)] + ([(kc_ref, vc_ref)] if has_cache else [])
        s_list = [_dot_nt(q, kr[:, lo:lo + GQA_HD]) * scale for kr, _ in parts]
        v_list = [vr[:, lo:lo + GQA_HD] for _, vr in parts]
        o_ref[:, h * GQA_HD:(h + 1) * GQA_HD] = _softmax_pv(s_list, v_list).astype(BF16)


def _attention(kernel, q, kv_list, cache_list, out_w, n_seq, seq_len, row0, past, name):
    tq = SEQ_TILE
    qb = seq_len // tq
    blk0 = row0 // tq
    seq0 = row0 // seq_len
    has_cache = cache_list is not None
    in_specs = [pl.BlockSpec((tq, q.shape[1]), lambda b, i: (blk0 + b * qb + i, 0))]
    in_specs += [pl.BlockSpec((seq_len, a.shape[1]), lambda b, i: (seq0 + b, 0)) for a in kv_list]
    args = [q] + list(kv_list)
    if has_cache:
        in_specs += [pl.BlockSpec((past, a.shape[1]), lambda b, i: (b, 0)) for a in cache_list]
        args += list(cache_list)
    return pl.pallas_call(
        functools.partial(kernel, has_cache=has_cache),
        out_shape=jax.ShapeDtypeStruct((n_seq * seq_len, out_w), BF16),
        grid=(n_seq, qb),
        in_specs=in_specs,
        out_specs=pl.BlockSpec((tq, out_w), lambda b, i: (b * qb + i, 0)),
        compiler_params=_params(("parallel", "parallel"), 48),
        name=name,
    )(*args)


def _gqa_prep_kernel(q_ref, k_ref, v_ref, gq_ref, gk_ref, cos_ref, sin_ref, qo_ref, ko_ref, vo_ref, kn_ref):
    cos, sin = cos_ref[...], sin_ref[...]
    for h in range(GQA_HEADS):
        sl = slice(h * GQA_HD, (h + 1) * GQA_HD)
        qo_ref[:, sl] = _rope(_rms(q_ref[:, sl], gq_ref[...]), cos, sin, GQA_HD // 4).astype(BF16)
    for h in range(GQA_KV_HEADS):
        sl = slice(h * GQA_HD, (h + 1) * GQA_HD)
        kn = _rms(k_ref[:, sl], gk_ref[...])
        kn_ref[:, sl] = kn
        ko_ref[:, sl] = _rope(kn, cos, sin, GQA_HD // 4).astype(BF16)
    vo_ref[...] = v_ref[...].astype(BF16)


def _fill_halo_buffer(buf, main, prev, nxt, first, last, halo):
    tm = main.shape[0]
    buf[pl.ds(halo, tm), :] = main
    buf[pl.ds(0, halo), :] = jnp.where(first, 0.0, prev)
    buf[pl.ds(halo + tm, halo), :] = jnp.where(last, 0.0, nxt)


def _depthwise(buf, w_ref, width, halo, tm, lo, w_lo, n):
    pad = width // 2
    acc = jnp.zeros((tm, n), F32)
    for k in range(width):
        acc = acc + w_ref[k:k + 1, w_lo:w_lo + n] * buf[pl.ds(halo - pad + k, tm), lo:lo + n]
    return acc


def _glu(x):
    return x[:, :CONV_CH] * jax.nn.sigmoid(x[:, CONV_CH:])


def _conformer_kernel(edge_ref, x_ref, xp_ref, xn_ref, w_ref, b_ref, lw_ref, lb_ref, o_ref, buf, ybuf):
    i = pl.program_id(0)
    first, last = edge_ref[0, i] == 1, edge_ref[1, i] == 1
    halo = xp_ref.shape[0]
    tm = x_ref.shape[0]
    _fill_halo_buffer(buf, _glu(x_ref[...]), _glu(xp_ref[...]), _glu(xn_ref[...]), first, last, halo)
    for c in range(CONV_CH // LANES):
        lo = c * LANES
        ybuf[:, lo:lo + LANES] = _depthwise(buf, w_ref, CONV_K, halo, tm, lo, lo, LANES) + b_ref[:, lo:lo + LANES]
    y = ybuf[...]
    yc = y - jnp.mean(y, axis=-1, keepdims=True)
    yn = yc * lax.rsqrt(jnp.mean(yc * yc, axis=-1, keepdims=True) + EPS) * lw_ref[...] + lb_ref[...]
    o_ref[...] = _silu(yn).astype(BF16)


def _dn_prep_kernel(edge_ref, q_ref, k_ref, v_ref, qp_ref, kp_ref, vp_ref, qn_ref, kn_ref, vn_ref,
                    ab_ref, w_ref, alog_ref, dtb_ref, qo_ref, ko_ref, vo_ref, gb_ref, buf):
    i = pl.program_id(0)
    first, last = edge_ref[0, i] == 1, edge_ref[1, i] == 1
    halo = qp_ref.shape[0]
    tm = q_ref.shape[0]
    width = DN_HEADS * DN_DK
    groups = ((q_ref, qp_ref, qn_ref, qo_ref, DN_DK ** -0.5), (k_ref, kp_ref, kn_ref, ko_ref, 1.0),
              (v_ref, vp_ref, vn_ref, vo_ref, None))
    for s, (x_ref, xp_ref, xn_ref, o_ref, norm_scale) in enumerate(groups):
        _fill_halo_buffer(buf, x_ref[...], xp_ref[...], xn_ref[...], first, last, halo)
        for h in range(DN_HEADS):
            lo = h * DN_DK
            y = _silu(_depthwise(buf, w_ref, DN_CONV, halo, tm, lo, s * width + lo, DN_DK))
            if norm_scale is not None:
                y = y * (lax.rsqrt(jnp.sum(y * y, axis=-1, keepdims=True) + EPS) * norm_scale)
            o_ref[:, lo:lo + DN_DK] = y
    x = ab_ref[...]
    lane = lax.broadcasted_iota(jnp.int32, x.shape, 1)
    z = x + dtb_ref[...]
    softplus = jnp.maximum(z, 0.0) + jnp.log(1.0 + jnp.exp(-jnp.abs(z)))
    g = -jnp.exp(alog_ref[...]) * softplus
    gb_ref[...] = jnp.where(lane < 2 * DN_HEADS, g, jax.nn.sigmoid(x))


def _split_bf16(x):
    hi = x.astype(BF16)
    return hi, (x - hi.astype(F32)).astype(BF16)


def _dot_split(a_hi, a_lo, b_hi, b_lo):
    m = a_hi.shape[0]
    top = _dot(jnp.concatenate([a_hi, a_lo], axis=0), b_hi)
    return top[:m] + top[m:] + _dot(a_hi, b_lo)


def _dn_scan_kernel(*refs, n_chunks, has_init, want_state):
    names = ["qf", "kf", "vf", "gf", "qb", "kb", "vb", "gb"]
    refs = list(refs)
    ins = {n: refs.pop(0) for n in names}
    s0_ref = refs.pop(0) if has_init else None
    of_ref, ob_ref = refs.pop(0), refs.pop(0)
    sout_ref = refs.pop(0) if want_state else None
    state = refs.pop(0)
    c = pl.program_id(1)
    n = DN_CHUNK

    @pl.when(c == 0)
    def _():
        if has_init:
            state[...] = s0_ref[...]
        else:
            state[...] = jnp.zeros(state.shape, F32)

    row = lax.broadcasted_iota(jnp.int32, (n, n), 0)
    col = lax.broadcasted_iota(jnp.int32, (n, n), 1)
    eye = jnp.where(row == col, 1.0, 0.0)
    probs = []
    for d, (qn, kn, vn, gn, o_ref) in enumerate((("qf", "kf", "vf", "gf", of_ref), ("qb", "kb", "vb", "gb", ob_ref))):
        incl = (row >= col) if d == 0 else (row <= col)
        strict = (row > col) if d == 0 else (row < col)
        gb = ins[gn][...]
        gcum = _dot(incl.astype(F32), gb, lax.Precision.HIGHEST)
        gcum_t = gcum.T
        gtot = jnp.sum(gb, axis=0, keepdims=True)
        for h in range(DN_HEADS):
            j = d * DN_HEADS + h
            sl = slice(h * DN_DK, (h + 1) * DN_DK)
            q, k, v = ins[qn][:, sl], ins[kn][:, sl], ins[vn][:, sl]
            gc = gcum[:, j:j + 1]
            beta = gb[:, 2 * DN_HEADS + j:2 * DN_HEADS + j + 1]
            decay = jnp.exp(jnp.where(incl, gc - gcum_t[j:j + 1, :], NEG_INF))
            probs.append(dict(j=j, sl=sl, o_ref=o_ref, q=q, k=k, v=v, gc=gc, beta=beta, decay=decay,
                              strict=strict, g_last=gtot[:, j:j + 1], kb=k * beta))

    for p in probs:
        a = _dot_nt(jnp.concatenate([p["kb"], p["q"]], axis=0), p["k"])
        p["P"] = -jnp.where(p["strict"], a[:n] * p["decay"], 0.0)
        p["qk"] = a[n:] * p["decay"]
        p["X"] = eye + p["P"]
    for p in probs:
        hi, lo = _split_bf16(p["P"])
        p["P"] = _dot_split(hi, lo, hi, lo)
    steps = int(math.log2(n)) - 1
    for it in range(steps):
        for p in probs:
            last = it == steps - 1
            y = p["X"] if last else jnp.concatenate([p["X"], p["P"]], axis=0)
            y_hi, y_lo = _split_bf16(y)
            p_hi, p_lo = _split_bf16(p["P"]) if last else (y_hi[n:], y_lo[n:])
            r = _dot_split(y_hi, y_lo, p_hi, p_lo)
            p["X"] = p["X"] + r[:n]
            if not last:
                p["P"] = r[n:]
    for p in probs:
        p["eg"] = jnp.exp(p["gc"])
        t_hi, t_lo = _split_bf16(p["X"])
        r_hi, r_lo = _split_bf16(jnp.concatenate([p["v"] * p["beta"], p["kb"] * p["eg"]], axis=1))
        p["sol"] = _dot_split(t_hi, t_lo, r_hi, r_lo)
    for p in probs:
        p["s"] = state[p["j"]]
    for p in probs:
        u, w = p["sol"][:, :DN_DV], p["sol"][:, DN_DV:]
        b = _dot(jnp.concatenate([w, p["q"] * p["eg"]], axis=0), p["s"])
        delta = u - b[:n]
        p["o"] = b[n:] + _dot(p["qk"], delta)
        p["s_new"] = p["s"] * jnp.exp(p["g_last"]) + _dot_tn(p["k"] * jnp.exp(p["g_last"] - p["gc"]), delta)
    for p in probs:
        p["o_ref"][:, p["sl"]] = p["o"]
        state[p["j"]] = p["s_new"]

    if want_state:
        @pl.when(c == n_chunks - 1)
        def _():
            sout_ref[...] = state[...]


def _dn_scan(q, k, v, gb, s0, n_seq, seq_len, row0, want_state, name):
    n_chunks = seq_len // DN_CHUNK
    blk0 = row0 // DN_CHUNK
    w = DN_HEADS * DN_DK
    nst = 2 * DN_HEADS
    fwd = lambda b, c: (blk0 + b * n_chunks + c, 0)
    bwd = lambda b, c: (blk0 + b * n_chunks + n_chunks - 1 - c, 0)
    ofwd = lambda b, c: (b * n_chunks + c, 0)
    obwd = lambda b, c: (b * n_chunks + n_chunks - 1 - c, 0)
    in_specs, args = [], []
    for imap in (fwd, bwd):
        in_specs += [pl.BlockSpec((DN_CHUNK, w), imap)] * 3 + [pl.BlockSpec((DN_CHUNK, LANES), imap)]
        args += [q, k, v, gb]
    if s0 is not None:
        in_specs.append(pl.BlockSpec((None, nst, DN_DK, DN_DV), lambda b, c: (b, 0, 0, 0)))
        args.append(s0)
    out_shape = [jax.ShapeDtypeStruct((n_seq * seq_len, w), F32)] * 2
    out_specs = [pl.BlockSpec((DN_CHUNK, w), ofwd), pl.BlockSpec((DN_CHUNK, w), obwd)]
    if want_state:
        out_shape.append(jax.ShapeDtypeStruct((n_seq, nst, DN_DK, DN_DV), F32))
        out_specs.append(pl.BlockSpec((None, nst, DN_DK, DN_DV), lambda b, c: (b, 0, 0, 0)))
    return pl.pallas_call(
        functools.partial(_dn_scan_kernel, n_chunks=n_chunks, has_init=s0 is not None, want_state=want_state),
        out_shape=out_shape,
        grid=(n_seq, n_chunks),
        in_specs=in_specs,
        out_specs=out_specs,
        scratch_shapes=[pltpu.VMEM((nst, DN_DK, DN_DV), F32)],
        compiler_params=_params(("parallel", "arbitrary"), 32),
        name=name,
    )(*args)


def _dn_post_kernel(of_ref, ob_ref, z_ref, g_ref, o_ref):
    for h in range(DN_HEADS):
        sl = slice(h * DN_DV, (h + 1) * DN_DV)
        o = _rms(of_ref[:, sl] + ob_ref[:, sl], g_ref[...])
        o_ref[:, sl] = (o * _silu(z_ref[:, sl])).astype(BF16)


def _merge_kernel(*refs):
    b_refs, w_refs, g_refs, o_ref = refs[0:4], refs[4:8], refs[8:12], refs[12]
    acc = None
    for b_ref, w_ref, g_ref in zip(b_refs, w_refs, g_refs):
        term = jax.nn.sigmoid(g_ref[...]) * _dot(b_ref[...], w_ref[...])
        acc = term if acc is None else acc + term
    o_ref[...] = acc.astype(BF16)


def _pack_halves(y):
    n = y.shape[1] // 2
    bits = lax.bitcast_convert_type(y.astype(BF16).astype(F32), jnp.uint32)
    return (bits[:, :n] >> 16) | (bits[:, n:] & jnp.uint32(0xFFFF0000))


def _unpack_halves(w):
    lo = lax.bitcast_convert_type(w << 16, F32)
    hi = lax.bitcast_convert_type(w & jnp.uint32(0xFFFF0000), F32)
    return lo, hi


def _ffn_norm_router_kernel(x_ref, g_ref, sc_ref, sh_ref, rw_ref, rb_ref,
                            h_ref, hp_ref, idx_ref, rank_ref, wts_ref, cnt_ref, carry, *, n_experts):
    i = pl.program_id(0)

    @pl.when(i == 0)
    def _():
        carry[...] = jnp.zeros(carry.shape, F32)

    h = _rms(x_ref[...], g_ref[...]) * (1.0 + sc_ref[...]) + sh_ref[...]
    h_ref[...] = h.astype(BF16)
    hp_ref[...] = _pack_halves(h)
    tm = h.shape[0]
    scores = jax.nn.sigmoid(_dot(h, rw_ref[...], lax.Precision.HIGHEST))
    lane = lax.broadcasted_iota(jnp.int32, scores.shape, 1)
    biased = jnp.where(lane < n_experts, scores + rb_ref[...], NEG_INF)
    chosen = []
    for _ in range(TOP_K):
        m = jnp.max(biased, axis=-1, keepdims=True)
        pick = jnp.min(jnp.where(biased == m, lane, LANES), axis=-1, keepdims=True)
        sel = lane == pick
        chosen.append((pick, sel))
        biased = jnp.where(sel, NEG_INF, biased)
    mask = functools.reduce(jnp.logical_or, [sel for _, sel in chosen])
    maskf = jnp.where(mask, 1.0, 0.0)
    picked = scores * maskf
    dense_w = picked / jnp.sum(picked, axis=-1, keepdims=True) * ROUTE_SCALE
    row = lax.broadcasted_iota(jnp.int32, (tm, tm), 0)
    col = lax.broadcasted_iota(jnp.int32, (tm, tm), 1)
    before = jnp.where(row > col, 1.0, 0.0).astype(BF16)
    rank = _dot(before, maskf.astype(BF16)) + carry[0:1, :]
    carry[...] = carry[...] + jnp.sum(maskf, axis=0, keepdims=True)
    idx8 = jnp.zeros(scores.shape, jnp.int32)
    rank8 = jnp.zeros(scores.shape, F32)
    wts8 = jnp.zeros(scores.shape, F32)
    for k, (pick, sel) in enumerate(chosen):
        here = lane == k
        idx8 = jnp.where(here, pick, idx8)
        rank8 = jnp.where(here, jnp.sum(jnp.where(sel, rank, 0.0), axis=-1, keepdims=True), rank8)
        wts8 = jnp.where(here, jnp.sum(jnp.where(sel, dense_w, 0.0), axis=-1, keepdims=True), wts8)
    idx_ref[...] = idx8
    rank_ref[...] = rank8.astype(jnp.int32)
    wts_ref[...] = wts8
    cnt_ref[...] = carry[...]


def _dispatch_kernel(slot_hbm, x_ref, xs_in, xs_hbm, slots, sem_s, sem):
    del xs_in
    i = pl.program_id(0)
    tm = x_ref.shape[0]
    cp = pltpu.make_async_copy(slot_hbm.at[i], slots, sem_s)
    cp.start()
    cp.wait()

    def row_copy(r, j):
        return pltpu.make_async_copy(x_ref.at[pl.ds(r, 1)], xs_hbm.at[pl.ds(slots[r * TOP_K + j], 1)], sem)

    def issue(r, carry):
        for j in range(TOP_K):
            row_copy(r, j).start()
        return carry

    lax.fori_loop(0, tm, issue, 0)

    def drain(r, carry):
        for j in range(TOP_K):
            row_copy(r, j).wait()
        return carry

    lax.fori_loop(0, tm, drain, 0)


def _expert_kernel(be_ref, used_ref, x_ref, wgu_ref, wd_ref, o_ref):
    b = pl.program_id(0)
    f = wd_ref.shape[0]
    half = x_ref.shape[1]

    @pl.when(used_ref[b] == 1)
    def _():
        x_lo, x_hi = _unpack_halves(x_ref[...])
        gu = _dot(x_lo.astype(BF16), wgu_ref[:half, :]) + _dot(x_hi.astype(BF16), wgu_ref[half:, :])
        act = (_silu(gu[:, :f]) * gu[:, f:]).astype(BF16)
        o_ref[...] = _pack_halves(_dot(act, wd_ref[...]))

    @pl.when(used_ref[b] == 0)
    def _():
        o_ref[...] = jnp.zeros(o_ref.shape, jnp.uint32)


def _combine_kernel(slot_hbm, ys_hbm, wts_ref, o_ref, slots, buf, sem_s, sem):
    i = pl.program_id(0)
    tm = o_ref.shape[0]
    half = buf.shape[2]
    cs = pltpu.make_async_copy(slot_hbm.at[i], slots, sem_s)
    cs.start()
    cs.wait()

    def row_copy(r, j):
        return pltpu.make_async_copy(ys_hbm.at[pl.ds(slots[r * TOP_K + j], 1)], buf.at[j, pl.ds(r, 1)], sem)

    def issue(r, carry):
        for j in range(TOP_K):
            row_copy(r, j).start()
        return carry

    lax.fori_loop(0, tm, issue, 0)

    def drain(r, carry):
        for j in range(TOP_K):
            row_copy(r, j).wait()
        return carry

    lax.fori_loop(0, tm, drain, 0)

    acc_lo = acc_hi = None
    for j in range(TOP_K):
        lo, hi = _unpack_halves(buf[j])
        wj = wts_ref[:, j:j + 1]
        acc_lo = wj * lo if acc_lo is None else acc_lo + wj * lo
        acc_hi = wj * hi if acc_hi is None else acc_hi + wj * hi
    o_ref[:, :half] = acc_lo
    o_ref[:, half:] = acc_hi


def _shared_up_kernel(h_ref, w_ref, o_ref):
    gu = _dot(h_ref[...], w_ref[...])
    f = gu.shape[1] // 2
    o_ref[...] = (_silu(gu[:, :f]) * gu[:, f:]).astype(BF16)


def _rope_tables(n_tok, n_rot, lo, width, n_identity):
    rows = n_tok // GRID_W
    row = jnp.repeat(jnp.arange(rows, dtype=F32), GRID_W)
    col = jnp.tile(jnp.arange(GRID_W, dtype=F32), rows)
    half = n_rot // 2
    inv_freq = ROPE_THETA ** (-jnp.arange(0, half, 2, dtype=F32) / half)
    ar, ac = row[:, None] * inv_freq[None, :], col[:, None] * inv_freq[None, :]
    cos = jnp.concatenate([jnp.cos(ar), jnp.cos(ar), jnp.cos(ac), jnp.cos(ac)], axis=1)
    sin = jnp.concatenate([-jnp.sin(ar), jnp.sin(ar), -jnp.sin(ac), jnp.sin(ac)], axis=1)
    cos = jnp.pad(cos, ((0, n_identity), (lo, width - lo - n_rot)), constant_values=1.0)
    cos = cos.at[n_tok:, :].set(1.0)
    sin = jnp.pad(sin, ((0, n_identity), (lo, width - lo - n_rot)))
    return cos, sin


def kernel(x_prompt, x_sample, cache_mla_kv, cache_mla_kr, cache_gqa_k, cache_gqa_v, state_dn, c, c_ctx,
           w_mod, b_mod, norm_mix, norm_ffn, w_in, mla_q_norm, mla_w_q_up, mla_kv_norm, mla_w_kv_up,
           conv_dw_w, conv_dw_b, conv_ln_w, conv_ln_b, gqa_q_norm, gqa_k_norm, dn_conv_w, dn_a_log,
           dn_dt_bias, dn_o_norm, w_branch, w_out, router_w, router_bias, exp_w_gu, exp_w_down,
           sh_w_gu, sh_w_down, final_norm):
    nb, seq, d = x_prompt.shape
    db, dseq, _ = x_sample.shape
    depth = w_mod.shape[0]
    past = cache_mla_kv.shape[2]
    n_exp, _, f2 = exp_w_gu.shape[1:]
    fexp = f2 // 2
    tp, ts = nb * seq, db * dseq
    t = tp + ts
    tm = SEQ_TILE
    assert seq == tm and past == tm and db + 1 <= 8 and tp % (2 * tm) == 0 and dseq % (2 * tm) == 0 and tp % dseq == 0
    npt, spt = tp // tm, dseq // tm
    n_tiles = t // tm

    def row_of(i):
        return jnp.where(i < npt, 0, 1 + (i - npt) // spt)

    def pos_of(i):
        return jnp.where(i < npt, spt, (i - npt) % spt)

    tile_ids = np.arange(n_tiles)
    in_sample = tile_ids >= npt
    edges = jnp.asarray(np.stack([np.where(in_sample, (tile_ids - npt) % spt == 0, True),
                                  np.where(in_sample, (tile_ids - npt) % spt == spt - 1, True)]).astype(np.int32))

    x = jnp.concatenate([x_prompt.reshape(tp, d), x_sample.reshape(ts, d)], axis=0)
    cond8 = jnp.zeros((8, d), F32).at[0].set(c_ctx).at[1:1 + db].set(c)

    cos_m, sin_m = _rope_tables(dseq, MLA_ROPE, MLA_NOPE, HEAD_W, tm)
    cos_k, sin_k = _rope_tables(dseq, MLA_ROPE, 0, LANES, tm)
    cos_g, sin_g = _rope_tables(dseq, GQA_HD, 0, GQA_HD, tm)

    sizes = (Q_LORA, KV_LORA, MLA_ROPE, 2 * CONV_CH, GQA_HEADS * GQA_HD, GQA_KV_HEADS * GQA_HD,
             GQA_KV_HEADS * GQA_HD, DN_HEADS * DN_DK, DN_HEADS * DN_DK, DN_HEADS * DN_DV, DN_HEADS * DN_DV,
             2 * DN_HEADS, 2 * DN_HEADS, N_BRANCH * d)
    names = ("q_c", "kv_c", "k_r", "glu", "g_q", "g_k", "g_v", "d_q", "d_k", "d_v", "d_z", "d_a", "d_b", "gates")
    src = dict(zip(names, zip(np.cumsum((0,) + sizes[:-1]).tolist(), sizes)))
    order = ("gates", "glu", "g_q", "d_q", "d_k", "d_v", "d_z", "kv_c", "g_k", "g_v", "q_c", "k_r", "d_ab")
    col = {}
    off = 0
    src["d_ab"] = (src["d_a"][0], 4 * DN_HEADS)
    for name in order:
        width = max(LANES, src[name][1])
        off = -(-off // width) * width
        col[name] = (off, width)
        off += width
    p_width = -(-off // 512) * 512

    def arrange_w_in(w):
        parts, pos = [], 0
        for name in order:
            a, n = src[name]
            parts.append(jnp.pad(w[:, a:a + n], ((0, 0), (col[name][0] - pos, col[name][1] - n))))
            pos = col[name][0] + col[name][1]
        parts.append(jnp.zeros((w.shape[0], p_width - pos), w.dtype))
        return jnp.concatenate(parts, axis=1).astype(BF16)

    def head_pad(w, n_in, widths, pads):
        h = w.shape[1] // sum(widths)
        w3 = w.reshape(n_in, h, sum(widths))
        parts, a = [], 0
        for wd, pd in zip(widths, pads):
            parts.append(jnp.pad(w3[:, :, a:a + wd], ((0, 0), (0, 0), (0, pd))))
            a += wd
        return jnp.concatenate(parts, axis=2).reshape(n_in, -1).astype(BF16)

    def pcol(name, width=None):
        a, n = col[name]
        width = n if width is None else width
        assert a % width == 0
        return a // width

    w_out_b, w_branch_b = w_out.astype(BF16), w_branch.astype(BF16)
    wgu_b, wdw_b = exp_w_gu.astype(BF16), exp_w_down.astype(BF16)
    sh_gu_b, sh_dw_b = sh_w_gu.astype(BF16), sh_w_down.astype(BF16)
    wkv_b = mla_w_kv_up.astype(BF16)

    kv_lats, krs, gks, gvs, states = [], [], [], [], []
    for l in range(depth):
        mod4 = _modulation(cond8, w_mod, b_mod, l).reshape(8, 6, 1, d)
        h = _norm_mod(x, norm_mix, l, mod4, 1, 0, row_of)
        proj = _matmul(h, arrange_w_in(w_in[l])[None], 0, F32, "in_proj")

        wq = head_pad(mla_w_q_up[l], Q_LORA, (MLA_NOPE, MLA_ROPE), (0, HEAD_W - MLA_QK))
        q_mla = pl.pallas_call(
            _mla_q_kernel,
            out_shape=jax.ShapeDtypeStruct((t, MLA_HEADS * HEAD_W), BF16),
            grid=(n_tiles,),
            in_specs=[pl.BlockSpec((tm, Q_LORA), lambda i: (i, pcol("q_c"))),
                      pl.BlockSpec((None, 1, Q_LORA), lambda i: (l, 0, 0)),
                      pl.BlockSpec((Q_LORA, MLA_HEADS * HEAD_W), lambda i: (0, 0)),
                      pl.BlockSpec((tm, HEAD_W), lambda i: (pos_of(i), 0)),
                      pl.BlockSpec((tm, HEAD_W), lambda i: (pos_of(i), 0))],
            out_specs=pl.BlockSpec((tm, MLA_HEADS * HEAD_W), lambda i: (i, 0)),
            compiler_params=_params(("parallel",), 32),
            name="mla_q",
        )(proj, mla_q_norm.reshape(depth, 1, Q_LORA), wq, cos_m, sin_m)
        kv_lat, kv_up, kr_rot = pl.pallas_call(
            _mla_kv_kernel,
            out_shape=[jax.ShapeDtypeStruct((t, KV_LORA), F32),
                       jax.ShapeDtypeStruct((t, MLA_HEADS * HEAD_W), BF16),
                       jax.ShapeDtypeStruct((t, LANES), BF16)],
            grid=(n_tiles,),
            in_specs=[pl.BlockSpec((tm, KV_LORA), lambda i: (i, pcol("kv_c"))),
                      pl.BlockSpec((None, 1, KV_LORA), lambda i: (l, 0, 0)),
                      pl.BlockSpec((None, KV_LORA, MLA_HEADS * HEAD_W), lambda i: (l, 0, 0)),
                      pl.BlockSpec((tm, LANES), lambda i: (i, pcol("k_r"))),
                      pl.BlockSpec((tm, LANES), lambda i: (pos_of(i), 0)),
                      pl.BlockSpec((tm, LANES), lambda i: (pos_of(i), 0))],
            out_specs=[pl.BlockSpec((tm, KV_LORA), lambda i: (i, 0)),
                       pl.BlockSpec((tm, MLA_HEADS * HEAD_W), lambda i: (i, 0)),
                       pl.BlockSpec((tm, LANES), lambda i: (i, 0))],
            compiler_params=_params(("parallel",), 32),
            name="mla_kv",
        )(proj, mla_kv_norm.reshape(depth, 1, KV_LORA), wkv_b, proj, cos_k, sin_k)
        kv_up_c = _matmul(cache_mla_kv[:, l].reshape(db * past, KV_LORA).astype(BF16), wkv_b, l, BF16,
                          "mla_kv_cache")
        kr_c = jnp.pad(cache_mla_kr[:, l].reshape(db * past, MLA_ROPE), ((0, 0), (0, LANES - MLA_ROPE))).astype(BF16)
        o_a = jnp.concatenate([
            _attention(_mla_attn_kernel, q_mla, (kv_up, kr_rot), None, MLA_HEADS * MLA_V, nb, seq, 0, past,
                       "mla_attn_ctx"),
            _attention(_mla_attn_kernel, q_mla, (kv_up, kr_rot), (kv_up_c, kr_c), MLA_HEADS * MLA_V, db, dseq, tp,
                       past, "mla_attn_lat")], axis=0)

        halo_b = 16
        hb = tm // halo_b
        o_b = pl.pallas_call(
            _conformer_kernel,
            out_shape=jax.ShapeDtypeStruct((t, CONV_CH), BF16),
            grid_spec=pltpu.PrefetchScalarGridSpec(
                num_scalar_prefetch=1,
                grid=(n_tiles,),
                in_specs=[pl.BlockSpec((tm, 2 * CONV_CH), lambda i, e: (i, pcol("glu"))),
                          pl.BlockSpec((halo_b, 2 * CONV_CH),
                                       lambda i, e: (jnp.maximum(i * hb - 1, 0), pcol("glu"))),
                          pl.BlockSpec((halo_b, 2 * CONV_CH),
                                       lambda i, e: (jnp.minimum((i + 1) * hb, n_tiles * hb - 1), pcol("glu"))),
                          pl.BlockSpec((None, CONV_K, CONV_CH), lambda i, e: (l, 0, 0)),
                          pl.BlockSpec((None, 1, CONV_CH), lambda i, e: (l, 0, 0)),
                          pl.BlockSpec((None, 1, CONV_CH), lambda i, e: (l, 0, 0)),
                          pl.BlockSpec((None, 1, CONV_CH), lambda i, e: (l, 0, 0))],
                out_specs=pl.BlockSpec((tm, CONV_CH), lambda i, e: (i, 0)),
                scratch_shapes=[pltpu.VMEM((tm + 2 * halo_b, CONV_CH), F32), pltpu.VMEM((tm, CONV_CH), F32)]),
            compiler_params=_params(("parallel",), 32),
            name="conformer_conv",
        )(edges, proj, proj, proj, conv_dw_w, conv_dw_b.reshape(depth, 1, CONV_CH),
          conv_ln_w.reshape(depth, 1, CONV_CH), conv_ln_b.reshape(depth, 1, CONV_CH))

        wq_g, wk_g = GQA_HEADS * GQA_HD, GQA_KV_HEADS * GQA_HD
        q_g, k_g, v_g, k_norm = pl.pallas_call(
            _gqa_prep_kernel,
            out_shape=[jax.ShapeDtypeStruct((t, wq_g), BF16), jax.ShapeDtypeStruct((t, wk_g), BF16),
                       jax.ShapeDtypeStruct((t, wk_g), BF16), jax.ShapeDtypeStruct((t, wk_g), F32)],
            grid=(n_tiles,),
            in_specs=[pl.BlockSpec((tm, wq_g), lambda i: (i, pcol("g_q"))),
                      pl.BlockSpec((tm, wk_g), lambda i: (i, pcol("g_k"))),
                      pl.BlockSpec((tm, wk_g), lambda i: (i, pcol("g_v"))),
                      pl.BlockSpec((None, 1, GQA_HD), lambda i: (l, 0, 0)),
                      pl.BlockSpec((None, 1, GQA_HD), lambda i: (l, 0, 0)),
                      pl.BlockSpec((tm, GQA_HD), lambda i: (pos_of(i), 0)),
                      pl.BlockSpec((tm, GQA_HD), lambda i: (pos_of(i), 0))],
            out_specs=[pl.BlockSpec((tm, wq_g), lambda i: (i, 0)), pl.BlockSpec((tm, wk_g), lambda i: (i, 0)),
                       pl.BlockSpec((tm, wk_g), lambda i: (i, 0)), pl.BlockSpec((tm, wk_g), lambda i: (i, 0))],
            compiler_params=_params(("parallel",), 32),
            name="gqa_prep",
        )(proj, proj, proj, gqa_q_norm.reshape(depth, 1, GQA_HD), gqa_k_norm.reshape(depth, 1, GQA_HD),
          cos_g, sin_g)
        kc_g = cache_gqa_k[:, l].reshape(db * past, wk_g).astype(BF16)
        vc_g = cache_gqa_v[:, l].reshape(db * past, wk_g).astype(BF16)
        o_c = jnp.concatenate([
            _attention(_gqa_attn_kernel, q_g, (k_g, v_g), None, wq_g, nb, seq, 0, past, "gqa_attn_ctx"),
            _attention(_gqa_attn_kernel, q_g, (k_g, v_g), (kc_g, vc_g), wq_g, db, dseq, tp, past,
                       "gqa_attn_lat")], axis=0)

        halo_d = 8
        hd = tm // halo_d
        wdn = DN_HEADS * DN_DK
        prev_map = lambda name: (lambda i, e: (jnp.maximum(i * hd - 1, 0), pcol(name)))
        next_map = lambda name: (lambda i, e: (jnp.minimum((i + 1) * hd, n_tiles * hd - 1), pcol(name)))
        main_specs = [pl.BlockSpec((tm, wdn), (lambda name: (lambda i, e: (i, pcol(name))))(nm))
                      for nm in ("d_q", "d_k", "d_v")]
        prev_specs = [pl.BlockSpec((halo_d, wdn), prev_map(nm)) for nm in ("d_q", "d_k", "d_v")]
        next_specs = [pl.BlockSpec((halo_d, wdn), next_map(nm)) for nm in ("d_q", "d_k", "d_v")]
        pad16 = lambda a: jnp.pad(a.reshape(1, 2 * DN_HEADS), ((0, 0), (0, LANES - 2 * DN_HEADS)))
        qd, kd, vd, gbeta = pl.pallas_call(
            _dn_prep_kernel,
            out_shape=[jax.ShapeDtypeStruct((t, wdn), F32)] * 3 + [jax.ShapeDtypeStruct((t, LANES), F32)],
            grid_spec=pltpu.PrefetchScalarGridSpec(
                num_scalar_prefetch=1,
                grid=(n_tiles,),
                in_specs=main_specs + prev_specs + next_specs + [
                    pl.BlockSpec((tm, LANES), lambda i, e: (i, pcol("d_ab"))),
                    pl.BlockSpec((None, DN_CONV, 3 * wdn), lambda i, e: (l, 0, 0)),
                    pl.BlockSpec((1, LANES), lambda i, e: (0, 0)),
                    pl.BlockSpec((1, LANES), lambda i, e: (0, 0))],
                out_specs=[pl.BlockSpec((tm, wdn), lambda i, e: (i, 0))] * 3
                + [pl.BlockSpec((tm, LANES), lambda i, e: (i, 0))],
                scratch_shapes=[pltpu.VMEM((tm + 2 * halo_d, wdn), F32)]),
            compiler_params=_params(("parallel",), 40),
            name="dn_prep",
        )(edges, *([proj] * 10), dn_conv_w, pad16(dn_a_log[l]), pad16(dn_dt_bias[l]))
        of_p, ob_p, st_p = _dn_scan(qd, kd, vd, gbeta, None, nb, seq, 0, True, "dn_scan_ctx")
        s0 = state_dn[:, l].reshape(db, 2 * DN_HEADS, DN_DK, DN_DV)
        of_s, ob_s = _dn_scan(qd, kd, vd, gbeta, s0, db, dseq, tp, False, "dn_scan_lat")
        o_f = jnp.concatenate([of_p, of_s], axis=0)
        o_bw = jnp.concatenate([ob_p, ob_s], axis=0)
        o_d = pl.pallas_call(
            _dn_post_kernel,
            out_shape=jax.ShapeDtypeStruct((t, wdn), BF16),
            grid=(n_tiles,),
            in_specs=[pl.BlockSpec((tm, wdn), lambda i: (i, 0)), pl.BlockSpec((tm, wdn), lambda i: (i, 0)),
                      pl.BlockSpec((tm, wdn), lambda i: (i, pcol("d_z"))),
                      pl.BlockSpec((None, 1, DN_DV), lambda i: (l, 0, 0))],
            out_specs=pl.BlockSpec((tm, wdn), lambda i: (i, 0)),
            compiler_params=_params(("parallel",), 32),
            name="dn_post",
        )(o_f, o_bw, proj, dn_o_norm.reshape(depth, 1, DN_DV))

        tmm, tnm = _pick(t, 512), _pick(d, 512)
        g0 = col["gates"][0] // tnm
        wb = w_branch_b
        branch_specs = [pl.BlockSpec((tmm, BRANCH_W), lambda i, j: (i, 0))] * N_BRANCH
        w_specs = [pl.BlockSpec((None, None, BRANCH_W, tnm), (lambda b: (lambda i, j: (l, b, 0, j)))(b))
                   for b in range(N_BRANCH)]
        gate_specs = [pl.BlockSpec((tmm, tnm), (lambda b: (lambda i, j: (i, g0 + b * (d // tnm) + j)))(b))
                      for b in range(N_BRANCH)]
        merged = pl.pallas_call(
            _merge_kernel,
            out_shape=jax.ShapeDtypeStruct((t, d), BF16),
            grid=(t // tmm, d // tnm),
            in_specs=branch_specs + w_specs + gate_specs,
            out_specs=pl.BlockSpec((tmm, tnm), lambda i, j: (i, j)),
            compiler_params=_params(("parallel", "parallel"), 48),
            name="branch_merge",
        )(o_a, o_b, o_c, o_d, wb, wb, wb, wb, proj, proj, proj, proj)
        x = _matmul_residual(merged, w_out_b, l, x, mod4, 2, row_of, None, "out_proj")

        n_pad = LANES - n_exp
        rw = jnp.pad(router_w[l], ((0, 0), (0, n_pad)))
        rb = jnp.pad(router_bias[l].reshape(1, n_exp), ((0, 0), (0, n_pad)))
        half = d // 2
        h2, h2_packed, idx8, rank8, wts8, counts = pl.pallas_call(
            functools.partial(_ffn_norm_router_kernel, n_experts=n_exp),
            out_shape=[jax.ShapeDtypeStruct((t, d), BF16), jax.ShapeDtypeStruct((t, half), jnp.uint32),
                       jax.ShapeDtypeStruct((t, LANES), jnp.int32),
                       jax.ShapeDtypeStruct((t, LANES), jnp.int32), jax.ShapeDtypeStruct((t, LANES), F32),
                       jax.ShapeDtypeStruct((8, LANES), F32)],
            grid=(n_tiles,),
            in_specs=[pl.BlockSpec((tm, d), lambda i: (i, 0)),
                      pl.BlockSpec((None, 1, d), lambda i: (l, 0, 0)),
                      pl.BlockSpec((None, None, 1, d), lambda i: (row_of(i), 4, 0, 0)),
                      pl.BlockSpec((None, None, 1, d), lambda i: (row_of(i), 3, 0, 0)),
                      pl.BlockSpec((d, LANES), lambda i: (0, 0)),
                      pl.BlockSpec((1, LANES), lambda i: (0, 0))],
            out_specs=[pl.BlockSpec((tm, d), lambda i: (i, 0)), pl.BlockSpec((tm, half), lambda i: (i, 0)),
                       pl.BlockSpec((tm, LANES), lambda i: (i, 0)),
                       pl.BlockSpec((tm, LANES), lambda i: (i, 0)), pl.BlockSpec((tm, LANES), lambda i: (i, 0)),
                       pl.BlockSpec((8, LANES), lambda i: (0, 0))],
            scratch_shapes=[pltpu.VMEM((8, LANES), F32)],
            compiler_params=_params(("arbitrary",), 48),
            name="ffn_norm_router",
        )(x, norm_ffn.reshape(depth, 1, d), mod4, mod4, rw, rb)

        bm = EXPERT_BLOCK
        n_blocks = (t * TOP_K + n_exp * (bm - 1) + bm - 1) // bm
        cnt = counts[0, :n_exp].astype(jnp.int32)
        padded = (cnt + bm - 1) // bm * bm
        pad_end = jnp.cumsum(padded)
        pad_start = pad_end - padded
        slot8 = (pad_start[idx8[:, :TOP_K]] + rank8[:, :TOP_K]).astype(jnp.int32)
        blk = jnp.arange(n_blocks, dtype=jnp.int32)
        block_e = jnp.minimum(jnp.sum((pad_end[None, :] <= blk[:, None] * bm).astype(jnp.int32), axis=1),
                              n_exp - 1).astype(jnp.int32)
        used = (blk * bm < pad_end[-1]).astype(jnp.int32)

        n_slots = n_blocks * bm
        xs = pl.pallas_call(
            _dispatch_kernel,
            out_shape=jax.ShapeDtypeStruct((n_slots, half), jnp.uint32),
            grid=(n_tiles,),
            in_specs=[pl.BlockSpec(memory_space=pl.ANY),
                      pl.BlockSpec((tm, half), lambda i: (i, 0)),
                      pl.BlockSpec(memory_space=pl.ANY)],
            out_specs=pl.BlockSpec(memory_space=pl.ANY),
            scratch_shapes=[pltpu.SMEM((tm * TOP_K,), jnp.int32), pltpu.SemaphoreType.DMA,
                            pltpu.SemaphoreType.DMA],
            input_output_aliases={2: 0},
            compiler_params=_params(("arbitrary",), 32),
            name="moe_dispatch",
        )(slot8.reshape(n_tiles, tm * TOP_K), h2_packed, jnp.zeros((n_slots, half), jnp.uint32))

        ys = pl.pallas_call(
            _expert_kernel,
            out_shape=jax.ShapeDtypeStruct((n_slots, half), jnp.uint32),
            grid_spec=pltpu.PrefetchScalarGridSpec(
                num_scalar_prefetch=2,
                grid=(n_blocks,),
                in_specs=[pl.BlockSpec((bm, half), lambda b, be, us: (b, 0)),
                          pl.BlockSpec((None, None, d, f2), lambda b, be, us: (l, be[b], 0, 0)),
                          pl.BlockSpec((None, None, fexp, d), lambda b, be, us: (l, be[b], 0, 0))],
                out_specs=pl.BlockSpec((bm, half), lambda b, be, us: (b, 0))),
            compiler_params=_params(("arbitrary",), 48),
            name="moe_experts",
        )(block_e, used, xs, wgu_b, wdw_b)

        tmc = 128
        y_routed = pl.pallas_call(
            _combine_kernel,
            out_shape=jax.ShapeDtypeStruct((t, d), F32),
            grid=(t // tmc,),
            in_specs=[pl.BlockSpec(memory_space=pl.ANY), pl.BlockSpec(memory_space=pl.ANY),
                      pl.BlockSpec((tmc, LANES), lambda i: (i, 0))],
            out_specs=pl.BlockSpec((tmc, d), lambda i: (i, 0)),
            scratch_shapes=[pltpu.SMEM((tmc * TOP_K,), jnp.int32),
                            pltpu.VMEM((TOP_K, tmc, half), jnp.uint32),
                            pltpu.SemaphoreType.DMA, pltpu.SemaphoreType.DMA],
            compiler_params=_params(("arbitrary",), 40),
            name="moe_combine",
        )(slot8.reshape(t // tmc, tmc * TOP_K), ys, wts8)

        tms = _pick(t, 512)
        f2s = sh_w_gu.shape[2]
        act_sh = pl.pallas_call(
            _shared_up_kernel,
            out_shape=jax.ShapeDtypeStruct((t, f2s // 2), BF16),
            grid=(t // tms,),
            in_specs=[pl.BlockSpec((tms, d), lambda i: (i, 0)),
                      pl.BlockSpec((None, d, f2s), lambda i: (l, 0, 0))],
            out_specs=pl.BlockSpec((tms, f2s // 2), lambda i: (i, 0)),
            compiler_params=_params(("parallel",), 40),
            name="shared_up",
        )(h2, sh_gu_b)
        x = _matmul_residual(act_sh, sh_dw_b, l, x, mod4, 5, row_of, y_routed, "ffn_out")

        kv_lats.append(kv_lat[:tp].reshape(nb, seq, KV_LORA))
        a, n = col["k_r"][0], MLA_ROPE
        krs.append(proj[:tp, a:a + n].reshape(nb, seq, MLA_ROPE))
        gks.append(k_norm[:tp].reshape(nb, seq, GQA_KV_HEADS, GQA_HD))
        a, n = col["g_v"]
        gvs.append(proj[:tp, a:a + n].reshape(nb, seq, GQA_KV_HEADS, GQA_HD))
        states.append(st_p.reshape(nb, 2, DN_HEADS, DN_DK, DN_DV))

    y = _final_norm(x, final_norm)
    return (y[:tp].reshape(nb, seq, d), y[tp:].reshape(db, dseq, d),
            jnp.stack(kv_lats, axis=1), jnp.stack(krs, axis=1), jnp.stack(gks, axis=1),
            jnp.stack(gvs, axis=1), jnp.stack(states, axis=1))
```

```python
import functools
import math

import numpy as np
import jax
import jax.numpy as jnp
from jax import lax
from jax.experimental import pallas as pl
from jax.experimental.pallas import tpu as pltpu

F32 = jnp.float32
BF16 = jnp.bfloat16
EPS = 1e-6
NEG_INF = float("-inf")

GRID_W = 64
ROPE_THETA = 10000.0
MLA_HEADS, MLA_NOPE, MLA_ROPE, MLA_V = 8, 128, 64, 128
MLA_QK = MLA_NOPE + MLA_ROPE
Q_LORA, KV_LORA = 768, 512
CONV_CH, CONV_K = 1024, 31
GQA_HEADS, GQA_KV_HEADS, GQA_HD = 8, 2, 128
DN_HEADS, DN_DK, DN_DV, DN_CONV, DN_CHUNK = 8, 128, 128, 5, 64
N_BRANCH, BRANCH_W = 4, 1024
TOP_K = 8
ROUTE_SCALE = 2.5
LANES = 128
SEQ_TILE = 256
EXPERT_BLOCK = 256
HEAD_W = 2 * LANES


def _params(sem, vmem_mb):
    return pltpu.CompilerParams(dimension_semantics=sem, vmem_limit_bytes=vmem_mb << 20)


def _pick(n, pref):
    t = min(pref, n)
    while n % t:
        t //= 2
    return t


def _dot(a, b, precision=None):
    return jnp.dot(a, b, preferred_element_type=F32, precision=precision)


def _dot_nt(a, b):
    return lax.dot_general(a, b, (((1,), (1,)), ((), ())), preferred_element_type=F32)


def _dot_tn(a, b):
    return lax.dot_general(a, b, (((0,), (0,)), ((), ())), preferred_element_type=F32)


def _silu(x):
    return x * jax.nn.sigmoid(x)


def _rms(x, gain):
    return x * lax.rsqrt(jnp.mean(x * x, axis=-1, keepdims=True) + EPS) * gain


def _rope(x, cos, sin, quarter):
    n = x.shape[-1]
    lane = lax.broadcasted_iota(jnp.int32, x.shape, x.ndim - 1)
    first = (lane % (2 * quarter)) < quarter
    sw = jnp.where(first, pltpu.roll(x, n - quarter, x.ndim - 1), pltpu.roll(x, quarter, x.ndim - 1))
    return x * cos + sw * sin


def _mod_kernel(c_ref, w_ref, b_ref, o_ref):
    a = _silu(c_ref[...]).astype(BF16)
    o_ref[...] = _dot(a, w_ref[...].astype(BF16)) + b_ref[...]


def _modulation(cond8, w_mod, b_mod, l):
    d, n = w_mod.shape[1], w_mod.shape[2]
    tn = _pick(n, 512)
    return pl.pallas_call(
        _mod_kernel,
        out_shape=jax.ShapeDtypeStruct((8, n), F32),
        grid=(n // tn,),
        in_specs=[pl.BlockSpec((8, d), lambda j: (0, 0)),
                  pl.BlockSpec((None, d, tn), lambda j: (l, 0, j)),
                  pl.BlockSpec((None, 1, tn), lambda j: (l, 0, j))],
        out_specs=pl.BlockSpec((8, tn), lambda j: (0, j)),
        compiler_params=_params(("arbitrary",), 40),
        name="modulation",
    )(cond8, w_mod, b_mod.reshape(b_mod.shape[0], 1, n))


def _norm_mod_kernel(x_ref, g_ref, sc_ref, sh_ref, o_ref):
    y = _rms(x_ref[...], g_ref[...])
    o_ref[...] = (y * (1.0 + sc_ref[...]) + sh_ref[...]).astype(o_ref.dtype)


def _norm_mod(x, gain, l, mod4, k_scale, k_shift, row_of):
    t, d = x.shape
    tm = SEQ_TILE
    return pl.pallas_call(
        _norm_mod_kernel,
        out_shape=jax.ShapeDtypeStruct((t, d), BF16),
        grid=(t // tm,),
        in_specs=[pl.BlockSpec((tm, d), lambda i: (i, 0)),
                  pl.BlockSpec((None, 1, d), lambda i: (l, 0, 0)),
                  pl.BlockSpec((None, None, 1, d), lambda i: (row_of(i), k_scale, 0, 0)),
                  pl.BlockSpec((None, None, 1, d), lambda i: (row_of(i), k_shift, 0, 0))],
        out_specs=pl.BlockSpec((tm, d), lambda i: (i, 0)),
        compiler_params=_params(("parallel",), 32),
        name="norm_mod",
    )(x, gain.reshape(gain.shape[0], 1, d), mod4, mod4)


def _final_norm_kernel(x_ref, g_ref, o_ref):
    o_ref[...] = _rms(x_ref[...], g_ref[...])


def _final_norm(x, gain):
    t, d = x.shape
    tm = SEQ_TILE
    return pl.pallas_call(
        _final_norm_kernel,
        out_shape=jax.ShapeDtypeStruct((t, d), F32),
        grid=(t // tm,),
        in_specs=[pl.BlockSpec((tm, d), lambda i: (i, 0)),
                  pl.BlockSpec((1, d), lambda i: (0, 0))],
        out_specs=pl.BlockSpec((tm, d), lambda i: (i, 0)),
        compiler_params=_params(("parallel",), 32),
        name="final_norm",
    )(x, gain.reshape(1, d))


def _mm_kernel(a_ref, w_ref, o_ref):
    o_ref[...] = _dot(a_ref[...], w_ref[...]).astype(o_ref.dtype)


def _matmul(a, w, l, out_dtype, name, tm_pref=1024, tn_pref=512):
    m, k = a.shape
    n = w.shape[2]
    tm, tn = _pick(m, tm_pref), _pick(n, tn_pref)
    return pl.pallas_call(
        _mm_kernel,
        out_shape=jax.ShapeDtypeStruct((m, n), out_dtype),
        grid=(n // tn, m // tm),
        in_specs=[pl.BlockSpec((tm, k), lambda j, i: (i, 0)),
                  pl.BlockSpec((None, k, tn), lambda j, i: (l, 0, j))],
        out_specs=pl.BlockSpec((tm, tn), lambda j, i: (i, j)),
        compiler_params=_params(("parallel", "parallel"), 48),
        name=name,
    )(a, w)


def _mm_res_kernel(a_ref, w_ref, res_ref, gate_ref, *rest):
    o_ref = rest[-1]
    y = _dot(a_ref[...], w_ref[...])
    if len(rest) == 2:
        y = y + rest[0][...]
    o_ref[...] = res_ref[...] + gate_ref[...] * y


def _matmul_residual(a, w, l, res, mod4, k_gate, row_of, add, name):
    m, k = a.shape
    n = w.shape[2]
    tm, tn = SEQ_TILE * 2, _pick(n, 1024)
    row2 = lambda i: row_of(2 * i)
    in_specs = [pl.BlockSpec((tm, k), lambda j, i: (i, 0)),
                pl.BlockSpec((None, k, tn), lambda j, i: (l, 0, j)),
                pl.BlockSpec((tm, tn), lambda j, i: (i, j)),
                pl.BlockSpec((None, None, 1, tn), lambda j, i: (row2(i), k_gate, 0, j))]
    args = [a, w, res, mod4]
    if add is not None:
        in_specs.append(pl.BlockSpec((tm, tn), lambda j, i: (i, j)))
        args.append(add)
    return pl.pallas_call(
        _mm_res_kernel,
        out_shape=jax.ShapeDtypeStruct((m, n), F32),
        grid=(n // tn, m // tm),
        in_specs=in_specs,
        out_specs=pl.BlockSpec((tm, tn), lambda j, i: (i, j)),
        compiler_params=_params(("parallel", "parallel"), 48),
        name=name,
    )(*args)


def _mla_q_kernel(qc_ref, g_ref, w_ref, cos_ref, sin_ref, o_ref):
    xn = _rms(qc_ref[...], g_ref[...]).astype(BF16)
    q = _dot(xn, w_ref[...])
    cos, sin = cos_ref[...], sin_ref[...]
    for h in range(MLA_HEADS):
        qh = q[:, h * HEAD_W:(h + 1) * HEAD_W]
        o_ref[:, h * HEAD_W:(h + 1) * HEAD_W] = _rope(qh, cos, sin, MLA_ROPE // 4).astype(BF16)


def _mla_kv_kernel(kvc_ref, g_ref, w_ref, kr_ref, cos_ref, sin_ref, lat_ref, up_ref, kro_ref):
    lat = _rms(kvc_ref[...], g_ref[...])
    lat_ref[...] = lat
    up_ref[...] = _dot(lat.astype(BF16), w_ref[...]).astype(BF16)
    kro_ref[...] = _rope(kr_ref[...], cos_ref[...], sin_ref[...], MLA_ROPE // 4).astype(BF16)


def _softmax_pv(s_list, v_list):
    m = functools.reduce(jnp.maximum, [jnp.max(s, axis=-1, keepdims=True) for s in s_list])
    p_list = [jnp.exp(s - m) for s in s_list]
    den = functools.reduce(jnp.add, [jnp.sum(p, axis=-1, keepdims=True) for p in p_list])
    acc = functools.reduce(jnp.add, [_dot(p.astype(BF16), v) for p, v in zip(p_list, v_list)])
    return acc / den


def _mla_attn_kernel(*refs, has_cache):
    if has_cache:
        q_ref, kv_ref, kr_ref, kvc_ref, krc_ref, o_ref = refs
    else:
        q_ref, kv_ref, kr_ref, o_ref = refs
    scale = MLA_QK ** -0.5
    for h in range(MLA_HEADS):
        lo = h * HEAD_W
        qn = q_ref[:, lo:lo + LANES]
        qr = q_ref[:, lo + LANES:lo + HEAD_W]
        parts = [(kv_ref, kr_ref)] + ([(kvc_ref, krc_ref)] if has_cache else [])
        s_list, v_list = [], []
        for kvr, krr in parts:
            s = _dot_nt(qn, kvr[:, lo:lo + LANES]) + _dot_nt(qr, krr[...])
            s_list.append(s * scale)
            v_list.append(kvr[:, lo + LANES:lo + HEAD_W])
        o_ref[:, h * MLA_V:(h + 1) * MLA_V] = _softmax_pv(s_list, v_list).astype(BF16)


def _gqa_attn_kernel(*refs, has_cache):
    if has_cache:
        q_ref, k_ref, v_ref, kc_ref, vc_ref, o_ref = refs
    else:
        q_ref, k_ref, v_ref, o_ref = refs
    scale = GQA_HD ** -0.5
    grp = GQA_HEADS // GQA_KV_HEADS
    for h in range(GQA_HEADS):
        lo = (h // grp) * GQA_HD
        q = q_ref[:, h * GQA_HD:(h + 1) * GQA_HD]
        parts = [(k_ref, v_ref)] + ([(kc_ref, vc_ref)] if has_cache else [])
        s_list = [_dot_nt(q, kr[:, lo:lo + GQA_HD]) * scale for kr, _ in parts]
        v_list = [vr[:, lo:lo + GQA_HD] for _, vr in parts]
        o_ref[:, h * GQA_HD:(h + 1) * GQA_HD] = _softmax_pv(s_list, v_list).astype(BF16)


def _attention(kernel, q, kv_list, cache_list, out_w, n_seq, seq_len, row0, past, name):
    tq = SEQ_TILE
    qb = seq_len // tq
    blk0 = row0 // tq
    seq0 = row0 // seq_len
    has_cache = cache_list is not None
    in_specs = [pl.BlockSpec((tq, q.shape[1]), lambda b, i: (blk0 + b * qb + i, 0))]
    in_specs += [pl.BlockSpec((seq_len, a.shape[1]), lambda b, i: (seq0 + b, 0)) for a in kv_list]
    args = [q] + list(kv_list)
    if has_cache:
        in_specs += [pl.BlockSpec((past, a.shape[1]), lambda b, i: (b, 0)) for a in cache_list]
        args += list(cache_list)
    return pl.pallas_call(
        functools.partial(kernel, has_cache=has_cache),
        out_shape=jax.ShapeDtypeStruct((n_seq * seq_len, out_w), BF16),
        grid=(n_seq, qb),
        in_specs=in_specs,
        out_specs=pl.BlockSpec((tq, out_w), lambda b, i: (b * qb + i, 0)),
        compiler_params=_params(("parallel", "parallel"), 48),
        name=name,
    )(*args)


def _gqa_prep_kernel(q_ref, k_ref, v_ref, gq_ref, gk_ref, cos_ref, sin_ref, qo_ref, ko_ref, vo_ref, kn_ref):
    cos, sin = cos_ref[...], sin_ref[...]
    for h in range(GQA_HEADS):
        sl = slice(h * GQA_HD, (h + 1) * GQA_HD)
        qo_ref[:, sl] = _rope(_rms(q_ref[:, sl], gq_ref[...]), cos, sin, GQA_HD // 4).astype(BF16)
    for h in range(GQA_KV_HEADS):
        sl = slice(h * GQA_HD, (h + 1) * GQA_HD)
        kn = _rms(k_ref[:, sl], gk_ref[...])
        kn_ref[:, sl] = kn
        ko_ref[:, sl] = _rope(kn, cos, sin, GQA_HD // 4).astype(BF16)
    vo_ref[...] = v_ref[...].astype(BF16)


def _fill_halo_buffer(buf, main, prev, nxt, first, last, halo):
    tm = main.shape[0]
    buf[pl.ds(halo, tm), :] = main
    buf[pl.ds(0, halo), :] = jnp.where(first, 0.0, prev)
    buf[pl.ds(halo + tm, halo), :] = jnp.where(last, 0.0, nxt)


def _depthwise(buf, w_ref, width, halo, tm, lo, w_lo, n):
    pad = width // 2
    acc = jnp.zeros((tm, n), F32)
    for k in range(width):
        acc = acc + w_ref[k:k + 1, w_lo:w_lo + n] * buf[pl.ds(halo - pad + k, tm), lo:lo + n]
    return acc


def _glu(x):
    return x[:, :CONV_CH] * jax.nn.sigmoid(x[:, CONV_CH:])


def _conformer_kernel(edge_ref, x_ref, xp_ref, xn_ref, w_ref, b_ref, lw_ref, lb_ref, o_ref, buf, ybuf):
    i = pl.program_id(0)
    first, last = edge_ref[0, i] == 1, edge_ref[1, i] == 1
    halo = xp_ref.shape[0]
    tm = x_ref.shape[0]
    _fill_halo_buffer(buf, _glu(x_ref[...]), _glu(xp_ref[...]), _glu(xn_ref[...]), first, last, halo)
    for c in range(CONV_CH // LANES):
        lo = c * LANES
        ybuf[:, lo:lo + LANES] = _depthwise(buf, w_ref, CONV_K, halo, tm, lo, lo, LANES) + b_ref[:, lo:lo + LANES]
    y = ybuf[...]
    yc = y - jnp.mean(y, axis=-1, keepdims=True)
    yn = yc * lax.rsqrt(jnp.mean(yc * yc, axis=-1, keepdims=True) + EPS) * lw_ref[...] + lb_ref[...]
    o_ref[...] = _silu(yn).astype(BF16)


def _dn_prep_kernel(edge_ref, q_ref, k_ref, v_ref, qp_ref, kp_ref, vp_ref, qn_ref, kn_ref, vn_ref,
                    ab_ref, w_ref, alog_ref, dtb_ref, qo_ref, ko_ref, vo_ref, gb_ref, buf):
    i = pl.program_id(0)
    first, last = edge_ref[0, i] == 1, edge_ref[1, i] == 1
    halo = qp_ref.shape[0]
    tm = q_ref.shape[0]
    width = DN_HEADS * DN_DK
    groups = ((q_ref, qp_ref, qn_ref, qo_ref, DN_DK ** -0.5), (k_ref, kp_ref, kn_ref, ko_ref, 1.0),
              (v_ref, vp_ref, vn_ref, vo_ref, None))
    for s, (x_ref, xp_ref, xn_ref, o_ref, norm_scale) in enumerate(groups):
        _fill_halo_buffer(buf, x_ref[...], xp_ref[...], xn_ref[...], first, last, halo)
        for h in range(DN_HEADS):
            lo = h * DN_DK
            y = _silu(_depthwise(buf, w_ref, DN_CONV, halo, tm, lo, s * width + lo, DN_DK))
            if norm_scale is not None:
                y = y * (lax.rsqrt(jnp.sum(y * y, axis=-1, keepdims=True) + EPS) * norm_scale)
            o_ref[:, lo:lo + DN_DK] = y
    x = ab_ref[...]
    lane = lax.broadcasted_iota(jnp.int32, x.shape, 1)
    z = x + dtb_ref[...]
    softplus = jnp.maximum(z, 0.0) + jnp.log(1.0 + jnp.exp(-jnp.abs(z)))
    g = -jnp.exp(alog_ref[...]) * softplus
    gb_ref[...] = jnp.where(lane < 2 * DN_HEADS, g, jax.nn.sigmoid(x))


def _split_bf16(x):
    hi = x.astype(BF16)
    return hi, (x - hi.astype(F32)).astype(BF16)


def _dot_split(a_hi, a_lo, b_hi, b_lo):
    m = a_hi.shape[0]
    top = _dot(jnp.concatenate([a_hi, a_lo], axis=0), b_hi)
    return top[:m] + top[m:] + _dot(a_hi, b_lo)


def _dn_scan_kernel(*refs, n_chunks, has_init, want_state):
    names = ["qf", "kf", "vf", "gf", "qb", "kb", "vb", "gb"]
    refs = list(refs)
    ins = {n: refs.pop(0) for n in names}
    s0_ref = refs.pop(0) if has_init else None
    wsrc_ref = refs.pop(0)
    of_ref, ob_ref = refs.pop(0), refs.pop(0)
    sout_ref = refs.pop(0) if want_state else None
    wdst_ref = refs.pop(0)
    state = refs.pop(0)
    c = pl.program_id(1)
    n = DN_CHUNK

    wdst_ref[...] = wsrc_ref[...].astype(BF16)

    @pl.when(c == 0)
    def _():
        if has_init:
            state[...] = s0_ref[...]
        else:
            state[...] = jnp.zeros(state.shape, F32)

    row = lax.broadcasted_iota(jnp.int32, (n, n), 0)
    col = lax.broadcasted_iota(jnp.int32, (n, n), 1)
    eye = jnp.where(row == col, 1.0, 0.0)
    probs = []
    for d, (qn, kn, vn, gn, o_ref) in enumerate((("qf", "kf", "vf", "gf", of_ref), ("qb", "kb", "vb", "gb", ob_ref))):
        incl = (row >= col) if d == 0 else (row <= col)
        strict = (row > col) if d == 0 else (row < col)
        gb = ins[gn][...]
        gcum = _dot(incl.astype(F32), gb, lax.Precision.HIGHEST)
        gcum_t = gcum.T
        gtot = jnp.sum(gb, axis=0, keepdims=True)
        for h in range(DN_HEADS):
            j = d * DN_HEADS + h
            sl = slice(h * DN_DK, (h + 1) * DN_DK)
            q, k, v = ins[qn][:, sl], ins[kn][:, sl], ins[vn][:, sl]
            gc = gcum[:, j:j + 1]
            beta = gb[:, 2 * DN_HEADS + j:2 * DN_HEADS + j + 1]
            decay = jnp.exp(jnp.where(incl, gc - gcum_t[j:j + 1, :], NEG_INF))
            probs.append(dict(j=j, sl=sl, o_ref=o_ref, q=q, k=k, v=v, gc=gc, beta=beta, decay=decay,
                              strict=strict, g_last=gtot[:, j:j + 1], kb=k * beta))

    for p in probs:
        a = _dot_nt(jnp.concatenate([p["kb"], p["q"]], axis=0), p["k"])
        p["P"] = -jnp.where(p["strict"], a[:n] * p["decay"], 0.0)
        p["qk"] = a[n:] * p["decay"]
        p["X"] = eye + p["P"]
    for p in probs:
        hi, lo = _split_bf16(p["P"])
        p["P"] = _dot_split(hi, lo, hi, lo)
    steps = int(math.log2(n)) - 1
    for it in range(steps):
        for p in probs:
            last = it == steps - 1
            y = p["X"] if last else jnp.concatenate([p["X"], p["P"]], axis=0)
            y_hi, y_lo = _split_bf16(y)
            p_hi, p_lo = _split_bf16(p["P"]) if last else (y_hi[n:], y_lo[n:])
            r = _dot_split(y_hi, y_lo, p_hi, p_lo)
            p["X"] = p["X"] + r[:n]
            if not last:
                p["P"] = r[n:]
    for p in probs:
        p["eg"] = jnp.exp(p["gc"])
        t_hi, t_lo = _split_bf16(p["X"])
        r_hi, r_lo = _split_bf16(jnp.concatenate([p["v"] * p["beta"], p["kb"] * p["eg"]], axis=1))
        p["sol"] = _dot_split(t_hi, t_lo, r_hi, r_lo)
    for p in probs:
        p["s"] = state[p["j"]]
    for p in probs:
        u, w = p["sol"][:, :DN_DV], p["sol"][:, DN_DV:]
        b = _dot(jnp.concatenate([w, p["q"] * p["eg"]], axis=0), p["s"])
        delta = u - b[:n]
        p["o"] = b[n:] + _dot(p["qk"], delta)
        p["s_new"] = p["s"] * jnp.exp(p["g_last"]) + _dot_tn(p["k"] * jnp.exp(p["g_last"] - p["gc"]), delta)
    for p in probs:
        p["o_ref"][:, p["sl"]] = p["o"]
        state[p["j"]] = p["s_new"]

    if want_state:
        @pl.when(c == n_chunks - 1)
        def _():
            sout_ref[...] = state[...]


def _dn_scan(q, k, v, gb, s0, w_src, l, n_seq, seq_len, row0, want_state, name):
    n_chunks = seq_len // DN_CHUNK
    blk0 = row0 // DN_CHUNK
    w = DN_HEADS * DN_DK
    nst = 2 * DN_HEADS
    n_steps = n_seq * n_chunks
    w_rows, w_cols = w_src.shape[1], w_src.shape[2]
    slab = w_rows // n_steps
    assert slab * n_steps == w_rows and slab % 16 == 0
    fwd = lambda b, c: (blk0 + b * n_chunks + c, 0)
    bwd = lambda b, c: (blk0 + b * n_chunks + n_chunks - 1 - c, 0)
    ofwd = lambda b, c: (b * n_chunks + c, 0)
    obwd = lambda b, c: (b * n_chunks + n_chunks - 1 - c, 0)
    in_specs, args = [], []
    for imap in (fwd, bwd):
        in_specs += [pl.BlockSpec((DN_CHUNK, w), imap)] * 3 + [pl.BlockSpec((DN_CHUNK, LANES), imap)]
        args += [q, k, v, gb]
    if s0 is not None:
        in_specs.append(pl.BlockSpec((None, nst, DN_DK, DN_DV), lambda b, c: (b, 0, 0, 0)))
        args.append(s0)
    in_specs.append(pl.BlockSpec((None, slab, w_cols), lambda b, c: (l, b * n_chunks + c, 0)))
    args.append(w_src)
    out_shape = [jax.ShapeDtypeStruct((n_seq * seq_len, w), F32)] * 2
    out_specs = [pl.BlockSpec((DN_CHUNK, w), ofwd), pl.BlockSpec((DN_CHUNK, w), obwd)]
    if want_state:
        out_shape.append(jax.ShapeDtypeStruct((n_seq, nst, DN_DK, DN_DV), F32))
        out_specs.append(pl.BlockSpec((None, nst, DN_DK, DN_DV), lambda b, c: (b, 0, 0, 0)))
    out_shape.append(jax.ShapeDtypeStruct((w_rows, w_cols), BF16))
    out_specs.append(pl.BlockSpec((slab, w_cols), lambda b, c: (b * n_chunks + c, 0)))
    return pl.pallas_call(
        functools.partial(_dn_scan_kernel, n_chunks=n_chunks, has_init=s0 is not None, want_state=want_state),
        out_shape=out_shape,
        grid=(n_seq, n_chunks),
        in_specs=in_specs,
        out_specs=out_specs,
        scratch_shapes=[pltpu.VMEM((nst, DN_DK, DN_DV), F32)],
        compiler_params=_params(("parallel", "arbitrary"), 48),
        name=name,
    )(*args)


def _dn_post_kernel(of_ref, ob_ref, z_ref, g_ref, o_ref):
    for h in range(DN_HEADS):
        sl = slice(h * DN_DV, (h + 1) * DN_DV)
        o = _rms(of_ref[:, sl] + ob_ref[:, sl], g_ref[...])
        o_ref[:, sl] = (o * _silu(z_ref[:, sl])).astype(BF16)


def _merge_kernel(*refs):
    b_refs, w_refs, g_refs, o_ref = refs[0:4], refs[4:8], refs[8:12], refs[12]
    acc = None
    for b_ref, w_ref, g_ref in zip(b_refs, w_refs, g_refs):
        term = jax.nn.sigmoid(g_ref[...]) * _dot(b_ref[...], w_ref[...])
        acc = term if acc is None else acc + term
    o_ref[...] = acc.astype(BF16)


def _pack_halves(y):
    n = y.shape[1] // 2
    bits = lax.bitcast_convert_type(y.astype(BF16).astype(F32), jnp.uint32)
    return (bits[:, :n] >> 16) | (bits[:, n:] & jnp.uint32(0xFFFF0000))


def _unpack_halves(w):
    lo = lax.bitcast_convert_type(w << 16, F32)
    hi = lax.bitcast_convert_type(w & jnp.uint32(0xFFFF0000), F32)
    return lo, hi


def _ffn_norm_router_kernel(x_ref, g_ref, sc_ref, sh_ref, rw_ref, rb_ref,
                            h_ref, hp_ref, idx_ref, rank_ref, wts_ref, cnt_ref, carry, *, n_experts):
    i = pl.program_id(0)

    @pl.when(i == 0)
    def _():
        carry[...] = jnp.zeros(carry.shape, F32)

    h = _rms(x_ref[...], g_ref[...]) * (1.0 + sc_ref[...]) + sh_ref[...]
    h_ref[...] = h.astype(BF16)
    hp_ref[...] = _pack_halves(h)
    tm = h.shape[0]
    scores = jax.nn.sigmoid(_dot_split(*_split_bf16(h), *_split_bf16(rw_ref[...])))
    lane = lax.broadcasted_iota(jnp.int32, scores.shape, 1)
    biased = jnp.where(lane < n_experts, scores + rb_ref[...], NEG_INF)
    chosen = []
    for _ in range(TOP_K):
        m = jnp.max(biased, axis=-1, keepdims=True)
        pick = jnp.min(jnp.where(biased == m, lane, LANES), axis=-1, keepdims=True)
        sel = lane == pick
        chosen.append((pick, sel))
        biased = jnp.where(sel, NEG_INF, biased)
    mask = functools.reduce(jnp.logical_or, [sel for _, sel in chosen])
    maskf = jnp.where(mask, 1.0, 0.0)
    picked = scores * maskf
    dense_w = picked / jnp.sum(picked, axis=-1, keepdims=True) * ROUTE_SCALE
    row = lax.broadcasted_iota(jnp.int32, (tm, tm), 0)
    col = lax.broadcasted_iota(jnp.int32, (tm, tm), 1)
    before = jnp.where(row > col, 1.0, 0.0).astype(BF16)
    rank = _dot(before, maskf.astype(BF16)) + carry[0:1, :]
    carry[...] = carry[...] + jnp.sum(maskf, axis=0, keepdims=True)
    idx8 = jnp.zeros(scores.shape, jnp.int32)
    rank8 = jnp.zeros(scores.shape, F32)
    wts8 = jnp.zeros(scores.shape, F32)
    for k, (pick, sel) in enumerate(chosen):
        here = lane == k
        idx8 = jnp.where(here, pick, idx8)
        rank8 = jnp.where(here, jnp.sum(jnp.where(sel, rank, 0.0), axis=-1, keepdims=True), rank8)
        wts8 = jnp.where(here, jnp.sum(jnp.where(sel, dense_w, 0.0), axis=-1, keepdims=True), wts8)
    idx_ref[...] = idx8
    rank_ref[...] = rank8.astype(jnp.int32)
    wts_ref[...] = wts8
    cnt_ref[...] = carry[...]


def _dispatch_kernel(tail_ref, has_ref, slot_hbm, x_ref, xs_hbm, slots, zbuf, sem_s, sem_z, sem, *, max_unused):
    i = pl.program_id(0)
    tm = x_ref.shape[0]
    bm = zbuf.shape[0]
    n_exp = has_ref.shape[0] - 1
    n_blocks = xs_hbm.shape[0] // bm

    @pl.when(i == 0)
    def _():
        zbuf[...] = jnp.zeros(zbuf.shape, zbuf.dtype)

        def zero_block(row0):
            return pltpu.make_async_copy(zbuf, xs_hbm.at[pl.ds(row0, bm)], sem_z)

        n_used = has_ref[n_exp]
        for start in (True, False):
            for e in range(n_exp):
                cp = zero_block(pl.multiple_of(tail_ref[e], bm))
                pl.when(has_ref[e] == 1)(cp.start if start else cp.wait)
            for k in range(max_unused):
                cp = zero_block((n_blocks - 1 - k) * bm)
                pl.when(n_blocks - 1 - k >= n_used)(cp.start if start else cp.wait)

    cp = pltpu.make_async_copy(slot_hbm.at[i], slots, sem_s)
    cp.start()
    cp.wait()

    def row_copy(r, j):
        return pltpu.make_async_copy(x_ref.at[pl.ds(r, 1)], xs_hbm.at[pl.ds(slots[r * TOP_K + j], 1)], sem)

    def issue(r, carry):
        for j in range(TOP_K):
            row_copy(r, j).start()
        return carry

    lax.fori_loop(0, tm, issue, 0)

    def drain(r, carry):
        for j in range(TOP_K):
            row_copy(r, j).wait()
        return carry

    lax.fori_loop(0, tm, drain, 0)


def _expert_kernel(be_ref, used_ref, xblk_ref, x_ref, wgu_ref, wd_ref, o_ref):
    del be_ref, xblk_ref
    b = pl.program_id(0)
    f = wd_ref.shape[0]
    half = x_ref.shape[1]

    @pl.when(used_ref[b] == 1)
    def _():
        x_lo, x_hi = _unpack_halves(x_ref[...])
        gu = _dot(x_lo.astype(BF16), wgu_ref[:half, :]) + _dot(x_hi.astype(BF16), wgu_ref[half:, :])
        act = (_silu(gu[:, :f]) * gu[:, f:]).astype(BF16)
        o_ref[...] = _pack_halves(_dot(act, wd_ref[...]))

    @pl.when(used_ref[b] == 0)
    def _():
        o_ref[...] = jnp.zeros(o_ref.shape, jnp.uint32)


def _combine_kernel(slot_hbm, ys_hbm, wts_ref, o_ref, slots, buf, sem_s, sem):
    i = pl.program_id(0)
    tm = o_ref.shape[0]
    half = buf.shape[2]
    cs = pltpu.make_async_copy(slot_hbm.at[i], slots, sem_s)
    cs.start()
    cs.wait()

    def row_copy(r, j):
        return pltpu.make_async_copy(ys_hbm.at[pl.ds(slots[r * TOP_K + j], 1)], buf.at[j, pl.ds(r, 1)], sem)

    def issue(r, carry):
        for j in range(TOP_K):
            row_copy(r, j).start()
        return carry

    lax.fori_loop(0, tm, issue, 0)

    def drain(r, carry):
        for j in range(TOP_K):
            row_copy(r, j).wait()
        return carry

    lax.fori_loop(0, tm, drain, 0)

    acc_lo = acc_hi = None
    for j in range(TOP_K):
        lo, hi = _unpack_halves(buf[j])
        wj = wts_ref[:, j:j + 1]
        acc_lo = wj * lo if acc_lo is None else acc_lo + wj * lo
        acc_hi = wj * hi if acc_hi is None else acc_hi + wj * hi
    o_ref[:, :half] = acc_lo
    o_ref[:, half:] = acc_hi


def _shared_up_kernel(h_ref, w_ref, o_ref):
    gu = _dot(h_ref[...], w_ref[...])
    f = gu.shape[1] // 2
    o_ref[...] = (_silu(gu[:, :f]) * gu[:, f:]).astype(BF16)


def _rope_tables(n_tok, n_rot, lo, width, n_identity):
    rows = n_tok // GRID_W
    row = jnp.repeat(jnp.arange(rows, dtype=F32), GRID_W)
    col = jnp.tile(jnp.arange(GRID_W, dtype=F32), rows)
    half = n_rot // 2
    inv_freq = ROPE_THETA ** (-jnp.arange(0, half, 2, dtype=F32) / half)
    ar, ac = row[:, None] * inv_freq[None, :], col[:, None] * inv_freq[None, :]
    cos = jnp.concatenate([jnp.cos(ar), jnp.cos(ar), jnp.cos(ac), jnp.cos(ac)], axis=1)
    sin = jnp.concatenate([-jnp.sin(ar), jnp.sin(ar), -jnp.sin(ac), jnp.sin(ac)], axis=1)
    cos = jnp.pad(cos, ((0, n_identity), (lo, width - lo - n_rot)), constant_values=1.0)
    cos = cos.at[n_tok:, :].set(1.0)
    sin = jnp.pad(sin, ((0, n_identity), (lo, width - lo - n_rot)))
    return cos, sin


def kernel(x_prompt, x_sample, cache_mla_kv, cache_mla_kr, cache_gqa_k, cache_gqa_v, state_dn, c, c_ctx,
           w_mod, b_mod, norm_mix, norm_ffn, w_in, mla_q_norm, mla_w_q_up, mla_kv_norm, mla_w_kv_up,
           conv_dw_w, conv_dw_b, conv_ln_w, conv_ln_b, gqa_q_norm, gqa_k_norm, dn_conv_w, dn_a_log,
           dn_dt_bias, dn_o_norm, w_branch, w_out, router_w, router_bias, exp_w_gu, exp_w_down,
           sh_w_gu, sh_w_down, final_norm):
    nb, seq, d = x_prompt.shape
    db, dseq, _ = x_sample.shape
    depth = w_mod.shape[0]
    past = cache_mla_kv.shape[2]
    n_exp, _, f2 = exp_w_gu.shape[1:]
    fexp = f2 // 2
    tp, ts = nb * seq, db * dseq
    t = tp + ts
    tm = SEQ_TILE
    assert seq == tm and past == tm and db + 1 <= 8 and tp % (2 * tm) == 0 and dseq % (2 * tm) == 0 and tp % dseq == 0
    npt, spt = tp // tm, dseq // tm
    n_tiles = t // tm

    def row_of(i):
        return jnp.where(i < npt, 0, 1 + (i - npt) // spt)

    def pos_of(i):
        return jnp.where(i < npt, spt, (i - npt) % spt)

    tile_ids = np.arange(n_tiles)
    in_sample = tile_ids >= npt
    edges = jnp.asarray(np.stack([np.where(in_sample, (tile_ids - npt) % spt == 0, True),
                                  np.where(in_sample, (tile_ids - npt) % spt == spt - 1, True)]).astype(np.int32))

    x = jnp.concatenate([x_prompt.reshape(tp, d), x_sample.reshape(ts, d)], axis=0)
    cond8 = jnp.zeros((8, d), F32).at[0].set(c_ctx).at[1:1 + db].set(c)

    cos_m, sin_m = _rope_tables(dseq, MLA_ROPE, MLA_NOPE, HEAD_W, tm)
    cos_k, sin_k = _rope_tables(dseq, MLA_ROPE, 0, LANES, tm)
    cos_g, sin_g = _rope_tables(dseq, GQA_HD, 0, GQA_HD, tm)

    sizes = (Q_LORA, KV_LORA, MLA_ROPE, 2 * CONV_CH, GQA_HEADS * GQA_HD, GQA_KV_HEADS * GQA_HD,
             GQA_KV_HEADS * GQA_HD, DN_HEADS * DN_DK, DN_HEADS * DN_DK, DN_HEADS * DN_DV, DN_HEADS * DN_DV,
             2 * DN_HEADS, 2 * DN_HEADS, N_BRANCH * d)
    names = ("q_c", "kv_c", "k_r", "glu", "g_q", "g_k", "g_v", "d_q", "d_k", "d_v", "d_z", "d_a", "d_b", "gates")
    src = dict(zip(names, zip(np.cumsum((0,) + sizes[:-1]).tolist(), sizes)))
    order = ("gates", "glu", "g_q", "d_q", "d_k", "d_v", "d_z", "kv_c", "g_k", "g_v", "q_c", "k_r", "d_ab")
    col = {}
    off = 0
    src["d_ab"] = (src["d_a"][0], 4 * DN_HEADS)
    for name in order:
        width = max(LANES, src[name][1])
        off = -(-off // width) * width
        col[name] = (off, width)
        off += width
    p_width = -(-off // 512) * 512

    def arrange_w_in(w):
        parts, pos = [], 0
        for name in order:
            a, n = src[name]
            parts.append(jnp.pad(w[:, a:a + n], ((0, 0), (col[name][0] - pos, col[name][1] - n))))
            pos = col[name][0] + col[name][1]
        parts.append(jnp.zeros((w.shape[0], p_width - pos), w.dtype))
        return jnp.concatenate(parts, axis=1).astype(BF16)

    def head_pad(w, n_in, widths, pads):
        h = w.shape[1] // sum(widths)
        w3 = w.reshape(n_in, h, sum(widths))
        parts, a = [], 0
        for wd, pd in zip(widths, pads):
            parts.append(jnp.pad(w3[:, :, a:a + wd], ((0, 0), (0, 0), (0, pd))))
            a += wd
        return jnp.concatenate(parts, axis=2).reshape(n_in, -1).astype(BF16)

    def pcol(name, width=None):
        a, n = col[name]
        width = n if width is None else width
        assert a % width == 0
        return a // width

    w_out_b, w_branch_b = w_out.astype(BF16), w_branch.astype(BF16)
    sh_gu_b, sh_dw_b = sh_w_gu.astype(BF16), sh_w_down.astype(BF16)
    wkv_b = mla_w_kv_up.astype(BF16)

    kv_lats, krs, gks, gvs, states = [], [], [], [], []
    for l in range(depth):
        mod4 = _modulation(cond8, w_mod, b_mod, l).reshape(8, 6, 1, d)
        h = _norm_mod(x, norm_mix, l, mod4, 1, 0, row_of)
        proj = _matmul(h, arrange_w_in(w_in[l])[None], 0, F32, "in_proj")

        wq = head_pad(mla_w_q_up[l], Q_LORA, (MLA_NOPE, MLA_ROPE), (0, HEAD_W - MLA_QK))
        q_mla = pl.pallas_call(
            _mla_q_kernel,
            out_shape=jax.ShapeDtypeStruct((t, MLA_HEADS * HEAD_W), BF16),
            grid=(n_tiles,),
            in_specs=[pl.BlockSpec((tm, Q_LORA), lambda i: (i, pcol("q_c"))),
                      pl.BlockSpec((None, 1, Q_LORA), lambda i: (l, 0, 0)),
                      pl.BlockSpec((Q_LORA, MLA_HEADS * HEAD_W), lambda i: (0, 0)),
                      pl.BlockSpec((tm, HEAD_W), lambda i: (pos_of(i), 0)),
                      pl.BlockSpec((tm, HEAD_W), lambda i: (pos_of(i), 0))],
            out_specs=pl.BlockSpec((tm, MLA_HEADS * HEAD_W), lambda i: (i, 0)),
            compiler_params=_params(("parallel",), 32),
            name="mla_q",
        )(proj, mla_q_norm.reshape(depth, 1, Q_LORA), wq, cos_m, sin_m)
        kv_lat, kv_up, kr_rot = pl.pallas_call(
            _mla_kv_kernel,
            out_shape=[jax.ShapeDtypeStruct((t, KV_LORA), F32),
                       jax.ShapeDtypeStruct((t, MLA_HEADS * HEAD_W), BF16),
                       jax.ShapeDtypeStruct((t, LANES), BF16)],
            grid=(n_tiles,),
            in_specs=[pl.BlockSpec((tm, KV_LORA), lambda i: (i, pcol("kv_c"))),
                      pl.BlockSpec((None, 1, KV_LORA), lambda i: (l, 0, 0)),
                      pl.BlockSpec((None, KV_LORA, MLA_HEADS * HEAD_W), lambda i: (l, 0, 0)),
                      pl.BlockSpec((tm, LANES), lambda i: (i, pcol("k_r"))),
                      pl.BlockSpec((tm, LANES), lambda i: (pos_of(i), 0)),
                      pl.BlockSpec((tm, LANES), lambda i: (pos_of(i), 0))],
            out_specs=[pl.BlockSpec((tm, KV_LORA), lambda i: (i, 0)),
                       pl.BlockSpec((tm, MLA_HEADS * HEAD_W), lambda i: (i, 0)),
                       pl.BlockSpec((tm, LANES), lambda i: (i, 0))],
            compiler_params=_params(("parallel",), 32),
            name="mla_kv",
        )(proj, mla_kv_norm.reshape(depth, 1, KV_LORA), wkv_b, proj, cos_k, sin_k)
        kv_up_c = _matmul(cache_mla_kv[:, l].reshape(db * past, KV_LORA).astype(BF16), wkv_b, l, BF16,
                          "mla_kv_cache")
        kr_c = jnp.pad(cache_mla_kr[:, l].reshape(db * past, MLA_ROPE), ((0, 0), (0, LANES - MLA_ROPE))).astype(BF16)
        o_a = jnp.concatenate([
            _attention(_mla_attn_kernel, q_mla, (kv_up, kr_rot), None, MLA_HEADS * MLA_V, nb, seq, 0, past,
                       "mla_attn_ctx"),
            _attention(_mla_attn_kernel, q_mla, (kv_up, kr_rot), (kv_up_c, kr_c), MLA_HEADS * MLA_V, db, dseq, tp,
                       past, "mla_attn_lat")], axis=0)

        halo_b = 16
        hb = tm // halo_b
        o_b = pl.pallas_call(
            _conformer_kernel,
            out_shape=jax.ShapeDtypeStruct((t, CONV_CH), BF16),
            grid_spec=pltpu.PrefetchScalarGridSpec(
                num_scalar_prefetch=1,
                grid=(n_tiles,),
                in_specs=[pl.BlockSpec((tm, 2 * CONV_CH), lambda i, e: (i, pcol("glu"))),
                          pl.BlockSpec((halo_b, 2 * CONV_CH),
                                       lambda i, e: (jnp.maximum(i * hb - 1, 0), pcol("glu"))),
                          pl.BlockSpec((halo_b, 2 * CONV_CH),
                                       lambda i, e: (jnp.minimum((i + 1) * hb, n_tiles * hb - 1), pcol("glu"))),
                          pl.BlockSpec((None, CONV_K, CONV_CH), lambda i, e: (l, 0, 0)),
                          pl.BlockSpec((None, 1, CONV_CH), lambda i, e: (l, 0, 0)),
                          pl.BlockSpec((None, 1, CONV_CH), lambda i, e: (l, 0, 0)),
                          pl.BlockSpec((None, 1, CONV_CH), lambda i, e: (l, 0, 0))],
                out_specs=pl.BlockSpec((tm, CONV_CH), lambda i, e: (i, 0)),
                scratch_shapes=[pltpu.VMEM((tm + 2 * halo_b, CONV_CH), F32), pltpu.VMEM((tm, CONV_CH), F32)]),
            compiler_params=_params(("parallel",), 32),
            name="conformer_conv",
        )(edges, proj, proj, proj, conv_dw_w, conv_dw_b.reshape(depth, 1, CONV_CH),
          conv_ln_w.reshape(depth, 1, CONV_CH), conv_ln_b.reshape(depth, 1, CONV_CH))

        wq_g, wk_g = GQA_HEADS * GQA_HD, GQA_KV_HEADS * GQA_HD
        q_g, k_g, v_g, k_norm = pl.pallas_call(
            _gqa_prep_kernel,
            out_shape=[jax.ShapeDtypeStruct((t, wq_g), BF16), jax.ShapeDtypeStruct((t, wk_g), BF16),
                       jax.ShapeDtypeStruct((t, wk_g), BF16), jax.ShapeDtypeStruct((t, wk_g), F32)],
            grid=(n_tiles,),
            in_specs=[pl.BlockSpec((tm, wq_g), lambda i: (i, pcol("g_q"))),
                      pl.BlockSpec((tm, wk_g), lambda i: (i, pcol("g_k"))),
                      pl.BlockSpec((tm, wk_g), lambda i: (i, pcol("g_v"))),
                      pl.BlockSpec((None, 1, GQA_HD), lambda i: (l, 0, 0)),
                      pl.BlockSpec((None, 1, GQA_HD), lambda i: (l, 0, 0)),
                      pl.BlockSpec((tm, GQA_HD), lambda i: (pos_of(i), 0)),
                      pl.BlockSpec((tm, GQA_HD), lambda i: (pos_of(i), 0))],
            out_specs=[pl.BlockSpec((tm, wq_g), lambda i: (i, 0)), pl.BlockSpec((tm, wk_g), lambda i: (i, 0)),
                       pl.BlockSpec((tm, wk_g), lambda i: (i, 0)), pl.BlockSpec((tm, wk_g), lambda i: (i, 0))],
            compiler_params=_params(("parallel",), 32),
            name="gqa_prep",
        )(proj, proj, proj, gqa_q_norm.reshape(depth, 1, GQA_HD), gqa_k_norm.reshape(depth, 1, GQA_HD),
          cos_g, sin_g)
        kc_g = cache_gqa_k[:, l].reshape(db * past, wk_g).astype(BF16)
        vc_g = cache_gqa_v[:, l].reshape(db * past, wk_g).astype(BF16)
        o_c = jnp.concatenate([
            _attention(_gqa_attn_kernel, q_g, (k_g, v_g), None, wq_g, nb, seq, 0, past, "gqa_attn_ctx"),
            _attention(_gqa_attn_kernel, q_g, (k_g, v_g), (kc_g, vc_g), wq_g, db, dseq, tp, past,
                       "gqa_attn_lat")], axis=0)

        halo_d = 8
        hd = tm // halo_d
        wdn = DN_HEADS * DN_DK
        prev_map = lambda name: (lambda i, e: (jnp.maximum(i * hd - 1, 0), pcol(name)))
        next_map = lambda name: (lambda i, e: (jnp.minimum((i + 1) * hd, n_tiles * hd - 1), pcol(name)))
        main_specs = [pl.BlockSpec((tm, wdn), (lambda name: (lambda i, e: (i, pcol(name))))(nm))
                      for nm in ("d_q", "d_k", "d_v")]
        prev_specs = [pl.BlockSpec((halo_d, wdn), prev_map(nm)) for nm in ("d_q", "d_k", "d_v")]
        next_specs = [pl.BlockSpec((halo_d, wdn), next_map(nm)) for nm in ("d_q", "d_k", "d_v")]
        pad16 = lambda a: jnp.pad(a.reshape(1, 2 * DN_HEADS), ((0, 0), (0, LANES - 2 * DN_HEADS)))
        qd, kd, vd, gbeta = pl.pallas_call(
            _dn_prep_kernel,
            out_shape=[jax.ShapeDtypeStruct((t, wdn), F32)] * 3 + [jax.ShapeDtypeStruct((t, LANES), F32)],
            grid_spec=pltpu.PrefetchScalarGridSpec(
                num_scalar_prefetch=1,
                grid=(n_tiles,),
                in_specs=main_specs + prev_specs + next_specs + [
                    pl.BlockSpec((tm, LANES), lambda i, e: (i, pcol("d_ab"))),
                    pl.BlockSpec((None, DN_CONV, 3 * wdn), lambda i, e: (l, 0, 0)),
                    pl.BlockSpec((1, LANES), lambda i, e: (0, 0)),
                    pl.BlockSpec((1, LANES), lambda i, e: (0, 0))],
                out_specs=[pl.BlockSpec((tm, wdn), lambda i, e: (i, 0))] * 3
                + [pl.BlockSpec((tm, LANES), lambda i, e: (i, 0))],
                scratch_shapes=[pltpu.VMEM((tm + 2 * halo_d, wdn), F32)]),
            compiler_params=_params(("parallel",), 40),
            name="dn_prep",
        )(edges, *([proj] * 10), dn_conv_w, pad16(dn_a_log[l]), pad16(dn_dt_bias[l]))
        of_p, ob_p, st_p, wgu_l = _dn_scan(qd, kd, vd, gbeta, None, exp_w_gu.reshape(depth, n_exp * d, f2), l,
                                           nb, seq, 0, True, "dn_scan_ctx")
        s0 = state_dn[:, l].reshape(db, 2 * DN_HEADS, DN_DK, DN_DV)
        of_s, ob_s, wdw_l = _dn_scan(qd, kd, vd, gbeta, s0, exp_w_down.reshape(depth, n_exp * fexp, d), l,
                                     db, dseq, tp, False, "dn_scan_lat")
        wgu_l, wdw_l = wgu_l.reshape(n_exp, d, f2), wdw_l.reshape(n_exp, fexp, d)
        o_f = jnp.concatenate([of_p, of_s], axis=0)
        o_bw = jnp.concatenate([ob_p, ob_s], axis=0)
        o_d = pl.pallas_call(
            _dn_post_kernel,
            out_shape=jax.ShapeDtypeStruct((t, wdn), BF16),
            grid=(n_tiles,),
            in_specs=[pl.BlockSpec((tm, wdn), lambda i: (i, 0)), pl.BlockSpec((tm, wdn), lambda i: (i, 0)),
                      pl.BlockSpec((tm, wdn), lambda i: (i, pcol("d_z"))),
                      pl.BlockSpec((None, 1, DN_DV), lambda i: (l, 0, 0))],
            out_specs=pl.BlockSpec((tm, wdn), lambda i: (i, 0)),
            compiler_params=_params(("parallel",), 32),
            name="dn_post",
        )(o_f, o_bw, proj, dn_o_norm.reshape(depth, 1, DN_DV))

        tmm, tnm = _pick(t, 1024), _pick(d, 512)
        g0 = col["gates"][0] // tnm
        wb = w_branch_b
        branch_specs = [pl.BlockSpec((tmm, BRANCH_W), lambda i, j: (i, 0))] * N_BRANCH
        w_specs = [pl.BlockSpec((None, None, BRANCH_W, tnm), (lambda b: (lambda i, j: (l, b, 0, j)))(b))
                   for b in range(N_BRANCH)]
        gate_specs = [pl.BlockSpec((tmm, tnm), (lambda b: (lambda i, j: (i, g0 + b * (d // tnm) + j)))(b))
                      for b in range(N_BRANCH)]
        merged = pl.pallas_call(
            _merge_kernel,
            out_shape=jax.ShapeDtypeStruct((t, d), BF16),
            grid=(t // tmm, d // tnm),
            in_specs=branch_specs + w_specs + gate_specs,
            out_specs=pl.BlockSpec((tmm, tnm), lambda i, j: (i, j)),
            compiler_params=_params(("parallel", "parallel"), 56),
            name="branch_merge",
        )(o_a, o_b, o_c, o_d, wb, wb, wb, wb, proj, proj, proj, proj)
        x = _matmul_residual(merged, w_out_b, l, x, mod4, 2, row_of, None, "out_proj")

        n_pad = LANES - n_exp
        rw = jnp.pad(router_w[l], ((0, 0), (0, n_pad)))
        rb = jnp.pad(router_bias[l].reshape(1, n_exp), ((0, 0), (0, n_pad)))
        half = d // 2
        h2, h2_packed, idx8, rank8, wts8, counts = pl.pallas_call(
            functools.partial(_ffn_norm_router_kernel, n_experts=n_exp),
            out_shape=[jax.ShapeDtypeStruct((t, d), BF16), jax.ShapeDtypeStruct((t, half), jnp.uint32),
                       jax.ShapeDtypeStruct((t, LANES), jnp.int32),
                       jax.ShapeDtypeStruct((t, LANES), jnp.int32), jax.ShapeDtypeStruct((t, LANES), F32),
                       jax.ShapeDtypeStruct((8, LANES), F32)],
            grid=(n_tiles,),
            in_specs=[pl.BlockSpec((tm, d), lambda i: (i, 0)),
                      pl.BlockSpec((None, 1, d), lambda i: (l, 0, 0)),
                      pl.BlockSpec((None, None, 1, d), lambda i: (row_of(i), 4, 0, 0)),
                      pl.BlockSpec((None, None, 1, d), lambda i: (row_of(i), 3, 0, 0)),
                      pl.BlockSpec((d, LANES), lambda i: (0, 0)),
                      pl.BlockSpec((1, LANES), lambda i: (0, 0))],
            out_specs=[pl.BlockSpec((tm, d), lambda i: (i, 0)), pl.BlockSpec((tm, half), lambda i: (i, 0)),
                       pl.BlockSpec((tm, LANES), lambda i: (i, 0)),
                       pl.BlockSpec((tm, LANES), lambda i: (i, 0)), pl.BlockSpec((tm, LANES), lambda i: (i, 0)),
                       pl.BlockSpec((8, LANES), lambda i: (0, 0))],
            scratch_shapes=[pltpu.VMEM((8, LANES), F32)],
            compiler_params=_params(("arbitrary",), 48),
            name="ffn_norm_router",
        )(x, norm_ffn.reshape(depth, 1, d), mod4, mod4, rw, rb)

        bm = EXPERT_BLOCK
        n_blocks = (t * TOP_K + n_exp * (bm - 1) + bm - 1) // bm
        cnt = counts[0, :n_exp].astype(jnp.int32)
        padded = (cnt + bm - 1) // bm * bm
        pad_end = jnp.cumsum(padded)
        pad_start = pad_end - padded
        slot8 = (pad_start[idx8[:, :TOP_K]] + rank8[:, :TOP_K]).astype(jnp.int32)
        blk = jnp.arange(n_blocks, dtype=jnp.int32)
        block_e = jnp.minimum(jnp.sum((pad_end[None, :] <= blk[:, None] * bm).astype(jnp.int32), axis=1),
                              n_exp - 1).astype(jnp.int32)
        n_used = pad_end[-1] // bm
        used = (blk < n_used).astype(jnp.int32)
        x_blk = jnp.minimum(blk, n_used - 1).astype(jnp.int32)
        tail = (pad_end - bm).astype(jnp.int32)
        has = jnp.concatenate([(padded > 0).astype(jnp.int32), n_used.reshape(1).astype(jnp.int32)])
        max_unused = n_blocks - (t * TOP_K + bm - 1) // bm

        n_slots = n_blocks * bm
        xs = pl.pallas_call(
            functools.partial(_dispatch_kernel, max_unused=max_unused),
            out_shape=jax.ShapeDtypeStruct((n_slots, half), jnp.uint32),
            grid_spec=pltpu.PrefetchScalarGridSpec(
                num_scalar_prefetch=2,
                grid=(n_tiles,),
                in_specs=[pl.BlockSpec(memory_space=pl.ANY),
                          pl.BlockSpec((tm, half), lambda i, tl, hs: (i, 0))],
                out_specs=pl.BlockSpec(memory_space=pl.ANY),
                scratch_shapes=[pltpu.SMEM((tm * TOP_K,), jnp.int32), pltpu.VMEM((bm, half), jnp.uint32),
                                pltpu.SemaphoreType.DMA, pltpu.SemaphoreType.DMA, pltpu.SemaphoreType.DMA]),
            compiler_params=_params(("arbitrary",), 32),
            name="moe_dispatch",
        )(tail, has, slot8.reshape(n_tiles, tm * TOP_K), h2_packed)

        ys = pl.pallas_call(
            _expert_kernel,
            out_shape=jax.ShapeDtypeStruct((n_slots, half), jnp.uint32),
            grid_spec=pltpu.PrefetchScalarGridSpec(
                num_scalar_prefetch=3,
                grid=(n_blocks,),
                in_specs=[pl.BlockSpec((bm, half), lambda b, be, us, xb: (xb[b], 0)),
                          pl.BlockSpec((None, d, f2), lambda b, be, us, xb: (be[b], 0, 0)),
                          pl.BlockSpec((None, fexp, d), lambda b, be, us, xb: (be[b], 0, 0))],
                out_specs=pl.BlockSpec((bm, half), lambda b, be, us, xb: (b, 0))),
            compiler_params=_params(("arbitrary",), 48),
            name="moe_experts",
        )(block_e, used, x_blk, xs, wgu_l, wdw_l)

        tmc = 128
        y_routed = pl.pallas_call(
            _combine_kernel,
            out_shape=jax.ShapeDtypeStruct((t, d), F32),
            grid=(t // tmc,),
            in_specs=[pl.BlockSpec(memory_space=pl.ANY), pl.BlockSpec(memory_space=pl.ANY),
                      pl.BlockSpec((tmc, LANES), lambda i: (i, 0))],
            out_specs=pl.BlockSpec((tmc, d), lambda i: (i, 0)),
            scratch_shapes=[pltpu.SMEM((tmc * TOP_K,), jnp.int32),
                            pltpu.VMEM((TOP_K, tmc, half), jnp.uint32),
                            pltpu.SemaphoreType.DMA, pltpu.SemaphoreType.DMA],
            compiler_params=_params(("arbitrary",), 40),
            name="moe_combine",
        )(slot8.reshape(t // tmc, tmc * TOP_K), ys, wts8)

        tms = _pick(t, 512)
        f2s = sh_w_gu.shape[2]
        act_sh = pl.pallas_call(
            _shared_up_kernel,
            out_shape=jax.ShapeDtypeStruct((t, f2s // 2), BF16),
            grid=(t // tms,),
            in_specs=[pl.BlockSpec((tms, d), lambda i: (i, 0)),
                      pl.BlockSpec((None, d, f2s), lambda i: (l, 0, 0))],
            out_specs=pl.BlockSpec((tms, f2s // 2), lambda i: (i, 0)),
            compiler_params=_params(("parallel",), 40),
            name="shared_up",
        )(h2, sh_gu_b)
        x = _matmul_residual(act_sh, sh_dw_b, l, x, mod4, 5, row_of, y_routed, "ffn_out")

        kv_lats.append(kv_lat[:tp].reshape(nb, seq, KV_LORA))
        a, n = col["k_r"][0], MLA_ROPE
        krs.append(proj[:tp, a:a + n].reshape(nb, seq, MLA_ROPE))
        gks.append(k_norm[:tp].reshape(nb, seq, GQA_KV_HEADS, GQA_HD))
        a, n = col["g_v"]
        gvs.append(proj[:tp, a:a + n].reshape(nb, seq, GQA_KV_HEADS, GQA_HD))
        states.append(st_p.reshape(nb, 2, DN_HEADS, DN_DK, DN_DV))

    y = _final_norm(x, final_norm)
    return (y[:tp].reshape(nb, seq, d), y[tp:].reshape(db, dseq, d),
            jnp.stack(kv_lats, axis=1), jnp.stack(krs, axis=1), jnp.stack(gks, axis=1),
            jnp.stack(gvs, axis=1), jnp.stack(states, axis=1))
```

```python
import functools
import math

import numpy as np
import jax
import jax.numpy as jnp
from jax import lax
from jax.experimental import pallas as pl
from jax.experimental.pallas import tpu as pltpu

F32 = jnp.float32
BF16 = jnp.bfloat16
EPS = 1e-6
NEG_INF = float("-inf")

GRID_W = 64
ROPE_THETA = 10000.0
MLA_HEADS, MLA_NOPE, MLA_ROPE, MLA_V = 8, 128, 64, 128
MLA_QK = MLA_NOPE + MLA_ROPE
Q_LORA, KV_LORA = 768, 512
CONV_CH, CONV_K = 1024, 31
GQA_HEADS, GQA_KV_HEADS, GQA_HD = 8, 2, 128
DN_HEADS, DN_DK, DN_DV, DN_CONV, DN_CHUNK = 8, 128, 128, 5, 64
N_BRANCH, BRANCH_W = 4, 1024
TOP_K = 8
ROUTE_SCALE = 2.5
LANES = 128
SEQ_TILE = 256
EXPERT_BLOCK = 256
HEAD_W = 2 * LANES


def _params(sem, vmem_mb):
    return pltpu.CompilerParams(dimension_semantics=sem, vmem_limit_bytes=vmem_mb << 20)


def _pick(n, pref):
    t = min(pref, n)
    while n % t:
        t //= 2
    return t


def _dot(a, b, precision=None):
    return jnp.dot(a, b, preferred_element_type=F32, precision=precision)


def _dot_nt(a, b):
    return lax.dot_general(a, b, (((1,), (1,)), ((), ())), preferred_element_type=F32)


def _dot_tn(a, b):
    return lax.dot_general(a, b, (((0,), (0,)), ((), ())), preferred_element_type=F32)


def _silu(x):
    return x * jax.nn.sigmoid(x)


def _rms(x, gain):
    return x * lax.rsqrt(jnp.mean(x * x, axis=-1, keepdims=True) + EPS) * gain


def _rope(x, cos, sin, quarter):
    n = x.shape[-1]
    lane = lax.broadcasted_iota(jnp.int32, x.shape, x.ndim - 1)
    first = (lane % (2 * quarter)) < quarter
    sw = jnp.where(first, pltpu.roll(x, n - quarter, x.ndim - 1), pltpu.roll(x, quarter, x.ndim - 1))
    return x * cos + sw * sin


def _mod_kernel(c_ref, w_ref, b_ref, o_ref):
    a = _silu(c_ref[...]).astype(BF16)
    o_ref[...] = _dot(a, w_ref[...].astype(BF16)) + b_ref[...]


def _modulation(cond8, w_mod, b_mod, l):
    d, n = w_mod.shape[1], w_mod.shape[2]
    tn = _pick(n, 512)
    return pl.pallas_call(
        _mod_kernel,
        out_shape=jax.ShapeDtypeStruct((8, n), F32),
        grid=(n // tn,),
        in_specs=[pl.BlockSpec((8, d), lambda j: (0, 0)),
                  pl.BlockSpec((None, d, tn), lambda j: (l, 0, j)),
                  pl.BlockSpec((None, 1, tn), lambda j: (l, 0, j))],
        out_specs=pl.BlockSpec((8, tn), lambda j: (0, j)),
        compiler_params=_params(("arbitrary",), 40),
        name="modulation",
    )(cond8, w_mod, b_mod.reshape(b_mod.shape[0], 1, n))


def _norm_mod_kernel(x_ref, g_ref, sc_ref, sh_ref, o_ref):
    y = _rms(x_ref[...], g_ref[...])
    o_ref[...] = (y * (1.0 + sc_ref[...]) + sh_ref[...]).astype(o_ref.dtype)


def _norm_mod(x, gain, l, mod4, k_scale, k_shift, row_of):
    t, d = x.shape
    tm = SEQ_TILE
    return pl.pallas_call(
        _norm_mod_kernel,
        out_shape=jax.ShapeDtypeStruct((t, d), BF16),
        grid=(t // tm,),
        in_specs=[pl.BlockSpec((tm, d), lambda i: (i, 0)),
                  pl.BlockSpec((None, 1, d), lambda i: (l, 0, 0)),
                  pl.BlockSpec((None, None, 1, d), lambda i: (row_of(i), k_scale, 0, 0)),
                  pl.BlockSpec((None, None, 1, d), lambda i: (row_of(i), k_shift, 0, 0))],
        out_specs=pl.BlockSpec((tm, d), lambda i: (i, 0)),
        compiler_params=_params(("parallel",), 32),
        name="norm_mod",
    )(x, gain.reshape(gain.shape[0], 1, d), mod4, mod4)


def _final_norm_kernel(x_ref, g_ref, o_ref):
    o_ref[...] = _rms(x_ref[...], g_ref[...])


def _final_norm(x, gain):
    t, d = x.shape
    tm = SEQ_TILE
    return pl.pallas_call(
        _final_norm_kernel,
        out_shape=jax.ShapeDtypeStruct((t, d), F32),
        grid=(t // tm,),
        in_specs=[pl.BlockSpec((tm, d), lambda i: (i, 0)),
                  pl.BlockSpec((1, d), lambda i: (0, 0))],
        out_specs=pl.BlockSpec((tm, d), lambda i: (i, 0)),
        compiler_params=_params(("parallel",), 32),
        name="final_norm",
    )(x, gain.reshape(1, d))


def _mm_kernel(a_ref, w_ref, o_ref):
    o_ref[...] = _dot(a_ref[...], w_ref[...]).astype(o_ref.dtype)


def _matmul(a, w, l, out_dtype, name, tm_pref=1024, tn_pref=512):
    m, k = a.shape
    n = w.shape[2]
    tm, tn = _pick(m, tm_pref), _pick(n, tn_pref)
    return pl.pallas_call(
        _mm_kernel,
        out_shape=jax.ShapeDtypeStruct((m, n), out_dtype),
        grid=(n // tn, m // tm),
        in_specs=[pl.BlockSpec((tm, k), lambda j, i: (i, 0)),
                  pl.BlockSpec((None, k, tn), lambda j, i: (l, 0, j))],
        out_specs=pl.BlockSpec((tm, tn), lambda j, i: (i, j)),
        compiler_params=_params(("parallel", "parallel"), 48),
        name=name,
    )(a, w)


def _mm_res_kernel(a_ref, w_ref, res_ref, gate_ref, *rest):
    o_ref = rest[-1]
    y = _dot(a_ref[...], w_ref[...])
    if len(rest) == 2:
        y = y + rest[0][...]
    o_ref[...] = res_ref[...] + gate_ref[...] * y


def _matmul_residual(a, w, l, res, mod4, k_gate, row_of, add, name):
    m, k = a.shape
    n = w.shape[2]
    tm, tn = SEQ_TILE * 2, _pick(n, 1024)
    row2 = lambda i: row_of(2 * i)
    in_specs = [pl.BlockSpec((tm, k), lambda j, i: (i, 0)),
                pl.BlockSpec((None, k, tn), lambda j, i: (l, 0, j)),
                pl.BlockSpec((tm, tn), lambda j, i: (i, j)),
                pl.BlockSpec((None, None, 1, tn), lambda j, i: (row2(i), k_gate, 0, j))]
    args = [a, w, res, mod4]
    if add is not None:
        in_specs.append(pl.BlockSpec((tm, tn), lambda j, i: (i, j)))
        args.append(add)
    return pl.pallas_call(
        _mm_res_kernel,
        out_shape=jax.ShapeDtypeStruct((m, n), F32),
        grid=(n // tn, m // tm),
        in_specs=in_specs,
        out_specs=pl.BlockSpec((tm, tn), lambda j, i: (i, j)),
        compiler_params=_params(("parallel", "parallel"), 48),
        name=name,
    )(*args)


def _mla_q_kernel(qc_ref, g_ref, w_ref, cos_ref, sin_ref, o_ref):
    xn = _rms(qc_ref[...], g_ref[...]).astype(BF16)
    q = _dot(xn, w_ref[...])
    cos, sin = cos_ref[...], sin_ref[...]
    for h in range(MLA_HEADS):
        qh = q[:, h * HEAD_W:(h + 1) * HEAD_W]
        o_ref[:, h * HEAD_W:(h + 1) * HEAD_W] = _rope(qh, cos, sin, MLA_ROPE // 4).astype(BF16)


def _mla_kv_kernel(kvc_ref, g_ref, w_ref, kr_ref, cos_ref, sin_ref, lat_ref, up_ref, kro_ref):
    lat = _rms(kvc_ref[...], g_ref[...])
    lat_ref[...] = lat
    up_ref[...] = _dot(lat.astype(BF16), w_ref[...]).astype(BF16)
    kro_ref[...] = _rope(kr_ref[...], cos_ref[...], sin_ref[...], MLA_ROPE // 4).astype(BF16)


def _softmax_pv(s_list, v_list):
    m = functools.reduce(jnp.maximum, [jnp.max(s, axis=-1, keepdims=True) for s in s_list])
    p_list = [jnp.exp(s - m) for s in s_list]
    den = functools.reduce(jnp.add, [jnp.sum(p, axis=-1, keepdims=True) for p in p_list])
    acc = functools.reduce(jnp.add, [_dot(p.astype(BF16), v) for p, v in zip(p_list, v_list)])
    return acc / den


def _mla_attn_kernel(*refs, has_cache):
    if has_cache:
        q_ref, kv_ref, kr_ref, kvc_ref, krc_ref, o_ref = refs
    else:
        q_ref, kv_ref, kr_ref, o_ref = refs
    scale = MLA_QK ** -0.5
    for h in range(MLA_HEADS):
        lo = h * HEAD_W
        qn = q_ref[:, lo:lo + LANES]
        qr = q_ref[:, lo + LANES:lo + HEAD_W]
        parts = [(kv_ref, kr_ref)] + ([(kvc_ref, krc_ref)] if has_cache else [])
        s_list, v_list = [], []
        for kvr, krr in parts:
            s = _dot_nt(qn, kvr[:, lo:lo + LANES]) + _dot_nt(qr, krr[...])
            s_list.append(s * scale)
            v_list.append(kvr[:, lo + LANES:lo + HEAD_W])
        o_ref[:, h * MLA_V:(h + 1) * MLA_V] = _softmax_pv(s_list, v_list).astype(BF16)


def _gqa_attn_kernel(*refs, has_cache):
    if has_cache:
        q_ref, k_ref, v_ref, kc_ref, vc_ref, o_ref = refs
    else:
        q_ref, k_ref, v_ref, o_ref = refs
    scale = GQA_HD ** -0.5
    grp = GQA_HEADS // GQA_KV_HEADS
    for h in range(GQA_HEADS):
        lo = (h // grp) * GQA_HD
        q = q_ref[:, h * GQA_HD:(h + 1) * GQA_HD]
        parts = [(k_ref, v_ref)] + ([(kc_ref, vc_ref)] if has_cache else [])
        s_list = [_dot_nt(q, kr[:, lo:lo + GQA_HD]) * scale for kr, _ in parts]
        v_list = [vr[:, lo:lo + GQA_HD] for _, vr in parts]
        o_ref[:, h * GQA_HD:(h + 1) * GQA_HD] = _softmax_pv(s_list, v_list).astype(BF16)


def _attention(kernel, q, kv_list, cache_list, out_w, n_seq, seq_len, row0, past, name):
    tq = SEQ_TILE
    qb = seq_len // tq
    blk0 = row0 // tq
    seq0 = row0 // seq_len
    has_cache = cache_list is not None
    in_specs = [pl.BlockSpec((tq, q.shape[1]), lambda b, i: (blk0 + b * qb + i, 0))]
    in_specs += [pl.BlockSpec((seq_len, a.shape[1]), lambda b, i: (seq0 + b, 0)) for a in kv_list]
    args = [q] + list(kv_list)
    if has_cache:
        in_specs += [pl.BlockSpec((past, a.shape[1]), lambda b, i: (b, 0)) for a in cache_list]
        args += list(cache_list)
    return pl.pallas_call(
        functools.partial(kernel, has_cache=has_cache),
        out_shape=jax.ShapeDtypeStruct((n_seq * seq_len, out_w), BF16),
        grid=(n_seq, qb),
        in_specs=in_specs,
        out_specs=pl.BlockSpec((tq, out_w), lambda b, i: (b * qb + i, 0)),
        compiler_params=_params(("parallel", "parallel"), 48),
        name=name,
    )(*args)


def _gqa_prep_kernel(q_ref, k_ref, v_ref, gq_ref, gk_ref, cos_ref, sin_ref, qo_ref, ko_ref, vo_ref, kn_ref):
    cos, sin = cos_ref[...], sin_ref[...]
    for h in range(GQA_HEADS):
        sl = slice(h * GQA_HD, (h + 1) * GQA_HD)
        qo_ref[:, sl] = _rope(_rms(q_ref[:, sl], gq_ref[...]), cos, sin, GQA_HD // 4).astype(BF16)
    for h in range(GQA_KV_HEADS):
        sl = slice(h * GQA_HD, (h + 1) * GQA_HD)
        kn = _rms(k_ref[:, sl], gk_ref[...])
        kn_ref[:, sl] = kn
        ko_ref[:, sl] = _rope(kn, cos, sin, GQA_HD // 4).astype(BF16)
    vo_ref[...] = v_ref[...].astype(BF16)


def _fill_halo_buffer(buf, main, prev, nxt, first, last, halo):
    tm = main.shape[0]
    buf[pl.ds(halo, tm), :] = main
    buf[pl.ds(0, halo), :] = jnp.where(first, 0.0, prev)
    buf[pl.ds(halo + tm, halo), :] = jnp.where(last, 0.0, nxt)


def _depthwise(buf, w_ref, width, halo, tm, lo, w_lo, n):
    pad = width // 2
    acc = jnp.zeros((tm, n), F32)
    for k in range(width):
        acc = acc + w_ref[k:k + 1, w_lo:w_lo + n] * buf[pl.ds(halo - pad + k, tm), lo:lo + n]
    return acc


def _glu(x):
    return x[:, :CONV_CH] * jax.nn.sigmoid(x[:, CONV_CH:])


def _conformer_kernel(edge_ref, x_ref, xp_ref, xn_ref, w_ref, b_ref, lw_ref, lb_ref, o_ref, buf, ybuf):
    i = pl.program_id(0)
    first, last = edge_ref[0, i] == 1, edge_ref[1, i] == 1
    halo = xp_ref.shape[0]
    tm = x_ref.shape[0]
    _fill_halo_buffer(buf, _glu(x_ref[...]), _glu(xp_ref[...]), _glu(xn_ref[...]), first, last, halo)
    for c in range(CONV_CH // LANES):
        lo = c * LANES
        ybuf[:, lo:lo + LANES] = _depthwise(buf, w_ref, CONV_K, halo, tm, lo, lo, LANES) + b_ref[:, lo:lo + LANES]
    y = ybuf[...]
    yc = y - jnp.mean(y, axis=-1, keepdims=True)
    yn = yc * lax.rsqrt(jnp.mean(yc * yc, axis=-1, keepdims=True) + EPS) * lw_ref[...] + lb_ref[...]
    o_ref[...] = _silu(yn).astype(BF16)


def _dn_prep_kernel(edge_ref, q_ref, k_ref, v_ref, qp_ref, kp_ref, vp_ref, qn_ref, kn_ref, vn_ref,
                    ab_ref, w_ref, alog_ref, dtb_ref, qo_ref, ko_ref, vo_ref, gb_ref, buf):
    i = pl.program_id(0)
    first, last = edge_ref[0, i] == 1, edge_ref[1, i] == 1
    halo = qp_ref.shape[0]
    tm = q_ref.shape[0]
    width = DN_HEADS * DN_DK
    groups = ((q_ref, qp_ref, qn_ref, qo_ref, DN_DK ** -0.5), (k_ref, kp_ref, kn_ref, ko_ref, 1.0),
              (v_ref, vp_ref, vn_ref, vo_ref, None))
    for s, (x_ref, xp_ref, xn_ref, o_ref, norm_scale) in enumerate(groups):
        _fill_halo_buffer(buf, x_ref[...], xp_ref[...], xn_ref[...], first, last, halo)
        for h in range(DN_HEADS):
            lo = h * DN_DK
            y = _silu(_depthwise(buf, w_ref, DN_CONV, halo, tm, lo, s * width + lo, DN_DK))
            if norm_scale is not None:
                y = y * (lax.rsqrt(jnp.sum(y * y, axis=-1, keepdims=True) + EPS) * norm_scale)
            o_ref[:, lo:lo + DN_DK] = y
    x = ab_ref[...]
    lane = lax.broadcasted_iota(jnp.int32, x.shape, 1)
    z = x + dtb_ref[...]
    softplus = jnp.maximum(z, 0.0) + jnp.log(1.0 + jnp.exp(-jnp.abs(z)))
    g = -jnp.exp(alog_ref[...]) * softplus
    gb_ref[...] = jnp.where(lane < 2 * DN_HEADS, g, jax.nn.sigmoid(x))


def _split_bf16(x):
    hi = x.astype(BF16)
    return hi, (x - hi.astype(F32)).astype(BF16)


def _dot_split(a_hi, a_lo, b_hi, b_lo):
    m = a_hi.shape[0]
    top = _dot(jnp.concatenate([a_hi, a_lo], axis=0), b_hi)
    return top[:m] + top[m:] + _dot(a_hi, b_lo)


def _dn_scan_kernel(*refs, n_chunks, has_init, want_state):
    names = ["qf", "kf", "vf", "gf", "qb", "kb", "vb", "gb"]
    refs = list(refs)
    ins = {n: refs.pop(0) for n in names}
    s0_ref = refs.pop(0) if has_init else None
    wsrc_ref = refs.pop(0)
    of_ref, ob_ref = refs.pop(0), refs.pop(0)
    sout_ref = refs.pop(0) if want_state else None
    wdst_ref = refs.pop(0)
    state = refs.pop(0)
    c = pl.program_id(1)
    n = DN_CHUNK

    wdst_ref[...] = wsrc_ref[...].astype(BF16)

    @pl.when(c == 0)
    def _():
        if has_init:
            state[...] = s0_ref[...]
        else:
            state[...] = jnp.zeros(state.shape, F32)

    row = lax.broadcasted_iota(jnp.int32, (n, 2 * n), 0)
    lane = lax.broadcasted_iota(jnp.int32, (n, 2 * n), 1)
    col = lane % n
    left = lane < n
    eye = jnp.where(row == col, 1.0, 0.0)
    row1 = lax.broadcasted_iota(jnp.int32, (n, n), 0)
    col1 = lax.broadcasted_iota(jnp.int32, (n, n), 1)

    def block_diag(z):
        zero = jnp.zeros(z.shape, z.dtype)
        return jnp.concatenate([jnp.where(left, z, zero), jnp.where(left, zero, z)], axis=0)

    pairs = []
    for d, (qn, kn, vn, gn, o_ref) in enumerate((("qf", "kf", "vf", "gf", of_ref), ("qb", "kb", "vb", "gb", ob_ref))):
        incl = (row >= col) if d == 0 else (row <= col)
        strict = (row > col) if d == 0 else (row < col)
        incl1 = (row1 >= col1) if d == 0 else (row1 <= col1)
        gb = ins[gn][...]
        gcum = _dot(incl1.astype(F32), gb, lax.Precision.HIGHEST)
        gcum_t = gcum.T
        gtot = jnp.sum(gb, axis=0, keepdims=True)
        for h in range(0, DN_HEADS, 2):
            heads = []
            for hh in (h, h + 1):
                j = d * DN_HEADS + hh
                sl = slice(hh * DN_DK, (hh + 1) * DN_DK)
                k = ins[kn][:, sl]
                gc = gcum[:, j:j + 1]
                beta = gb[:, 2 * DN_HEADS + j:2 * DN_HEADS + j + 1]
                heads.append(dict(j=j, sl=sl, q=ins[qn][:, sl], k=k, v=ins[vn][:, sl], gc=gc, beta=beta,
                                  g_last=gtot[:, j:j + 1], kb=k * beta, eg=jnp.exp(gc)))
            j0 = d * DN_HEADS + h
            gc2 = jnp.where(left, heads[0]["gc"], heads[1]["gc"])
            gr2 = jnp.concatenate([gcum_t[j0:j0 + 1, :], gcum_t[j0 + 1:j0 + 2, :]], axis=1)
            decay = jnp.exp(jnp.where(incl, gc2 - gr2, NEG_INF))
            pairs.append(dict(heads=heads, o_ref=o_ref, decay=decay, strict=strict))

    for p in pairs:
        a = jnp.concatenate([_dot_nt(jnp.concatenate([hd["kb"], hd["q"]], axis=0), hd["k"]) for hd in p["heads"]],
                            axis=1)
        p["P"] = -jnp.where(p["strict"], a[:n] * p["decay"], 0.0)
        p["qk"] = a[n:] * p["decay"]
        p["X"] = eye + p["P"]
    for p in pairs:
        hi, lo = _split_bf16(p["P"])
        p["P"] = _dot_split(hi, lo, block_diag(hi), block_diag(lo))
    steps = int(math.log2(n)) - 1
    for it in range(steps):
        for p in pairs:
            last = it == steps - 1
            y = p["X"] if last else jnp.concatenate([p["X"], p["P"]], axis=0)
            y_hi, y_lo = _split_bf16(y)
            p_hi, p_lo = _split_bf16(p["P"]) if last else (y_hi[n:], y_lo[n:])
            r = _dot_split(y_hi, y_lo, block_diag(p_hi), block_diag(p_lo))
            p["X"] = p["X"] + r[:n]
            if not last:
                p["P"] = r[n:]
    for p in pairs:
        t_hi, t_lo = _split_bf16(p["X"])
        rhs = [jnp.concatenate([hd["v"] * hd["beta"], hd["kb"] * hd["eg"]], axis=1) for hd in p["heads"]]
        zero = jnp.zeros(rhs[0].shape, F32)
        r_hi, r_lo = _split_bf16(jnp.concatenate([jnp.concatenate([rhs[0], zero], axis=1),
                                                  jnp.concatenate([zero, rhs[1]], axis=1)], axis=0))
        p["sol"] = _dot_split(t_hi, t_lo, r_hi, r_lo)
    for p in pairs:
        for hd in p["heads"]:
            hd["s"] = state[hd["j"]]
    for p in pairs:
        deltas = []
        for i, hd in enumerate(p["heads"]):
            u = p["sol"][:, (2 * i) * DN_DV:(2 * i + 1) * DN_DV]
            w = p["sol"][:, (2 * i + 1) * DN_DV:(2 * i + 2) * DN_DV]
            b = _dot(jnp.concatenate([w, hd["q"] * hd["eg"]], axis=0), hd["s"])
            hd["qs"] = b[n:]
            deltas.append(u - b[:n])
            hd["s_new"] = hd["s"] * jnp.exp(hd["g_last"]) + _dot_tn(hd["k"] * jnp.exp(hd["g_last"] - hd["gc"]),
                                                                      deltas[i])
        zero = jnp.zeros(deltas[0].shape, F32)
        dd = jnp.concatenate([jnp.concatenate([deltas[0], zero], axis=1),
                              jnp.concatenate([zero, deltas[1]], axis=1)], axis=0)
        p["qkd"] = _dot(p["qk"], dd)
    for p in pairs:
        for i, hd in enumerate(p["heads"]):
            p["o_ref"][:, hd["sl"]] = hd["qs"] + p["qkd"][:, i * DN_DV:(i + 1) * DN_DV]
            state[hd["j"]] = hd["s_new"]

    if want_state:
        @pl.when(c == n_chunks - 1)
        def _():
            sout_ref[...] = state[...]


def _dn_scan(q, k, v, gb, s0, w_src, l, n_seq, seq_len, row0, want_state, name):
    n_chunks = seq_len // DN_CHUNK
    blk0 = row0 // DN_CHUNK
    w = DN_HEADS * DN_DK
    nst = 2 * DN_HEADS
    n_steps = n_seq * n_chunks
    w_rows, w_cols = w_src.shape[1], w_src.shape[2]
    slab = w_rows // n_steps
    assert slab * n_steps == w_rows and slab % 16 == 0
    fwd = lambda b, c: (blk0 + b * n_chunks + c, 0)
    bwd = lambda b, c: (blk0 + b * n_chunks + n_chunks - 1 - c, 0)
    ofwd = lambda b, c: (b * n_chunks + c, 0)
    obwd = lambda b, c: (b * n_chunks + n_chunks - 1 - c, 0)
    in_specs, args = [], []
    for imap in (fwd, bwd):
        in_specs += [pl.BlockSpec((DN_CHUNK, w), imap)] * 3 + [pl.BlockSpec((DN_CHUNK, LANES), imap)]
        args += [q, k, v, gb]
    if s0 is not None:
        in_specs.append(pl.BlockSpec((None, nst, DN_DK, DN_DV), lambda b, c: (b, 0, 0, 0)))
        args.append(s0)
    in_specs.append(pl.BlockSpec((None, slab, w_cols), lambda b, c: (l, b * n_chunks + c, 0)))
    args.append(w_src)
    out_shape = [jax.ShapeDtypeStruct((n_seq * seq_len, w), F32)] * 2
    out_specs = [pl.BlockSpec((DN_CHUNK, w), ofwd), pl.BlockSpec((DN_CHUNK, w), obwd)]
    if want_state:
        out_shape.append(jax.ShapeDtypeStruct((n_seq, nst, DN_DK, DN_DV), F32))
        out_specs.append(pl.BlockSpec((None, nst, DN_DK, DN_DV), lambda b, c: (b, 0, 0, 0)))
    out_shape.append(jax.ShapeDtypeStruct((w_rows, w_cols), BF16))
    out_specs.append(pl.BlockSpec((slab, w_cols), lambda b, c: (b * n_chunks + c, 0)))
    return pl.pallas_call(
        functools.partial(_dn_scan_kernel, n_chunks=n_chunks, has_init=s0 is not None, want_state=want_state),
        out_shape=out_shape,
        grid=(n_seq, n_chunks),
        in_specs=in_specs,
        out_specs=out_specs,
        scratch_shapes=[pltpu.VMEM((nst, DN_DK, DN_DV), F32)],
        compiler_params=_params(("parallel", "arbitrary"), 48),
        name=name,
    )(*args)


def _dn_post_kernel(of_ref, ob_ref, z_ref, g_ref, o_ref):
    for h in range(DN_HEADS):
        sl = slice(h * DN_DV, (h + 1) * DN_DV)
        o = _rms(of_ref[:, sl] + ob_ref[:, sl], g_ref[...])
        o_ref[:, sl] = (o * _silu(z_ref[:, sl])).astype(BF16)


def _merge_kernel(*refs):
    b_refs, w_refs, g_refs, o_ref = refs[0:4], refs[4:8], refs[8:12], refs[12]
    acc = None
    for b_ref, w_ref, g_ref in zip(b_refs, w_refs, g_refs):
        term = jax.nn.sigmoid(g_ref[...]) * _dot(b_ref[...], w_ref[...])
        acc = term if acc is None else acc + term
    o_ref[...] = acc.astype(BF16)


def _pack_halves(y):
    n = y.shape[1] // 2
    bits = lax.bitcast_convert_type(y.astype(BF16).astype(F32), jnp.uint32)
    return (bits[:, :n] >> 16) | (bits[:, n:] & jnp.uint32(0xFFFF0000))


def _unpack_halves(w):
    lo = lax.bitcast_convert_type(w << 16, F32)
    hi = lax.bitcast_convert_type(w & jnp.uint32(0xFFFF0000), F32)
    return lo, hi


def _ffn_norm_router_kernel(x_ref, g_ref, sc_ref, sh_ref, rw_ref, rb_ref,
                            h_ref, hp_ref, idx_ref, rank_ref, wts_ref, cnt_ref, carry, *, n_experts):
    i = pl.program_id(0)

    @pl.when(i == 0)
    def _():
        carry[...] = jnp.zeros(carry.shape, F32)

    h = _rms(x_ref[...], g_ref[...]) * (1.0 + sc_ref[...]) + sh_ref[...]
    h_ref[...] = h.astype(BF16)
    hp_ref[...] = _pack_halves(h)
    tm = h.shape[0]
    scores = jax.nn.sigmoid(_dot_split(*_split_bf16(h), *_split_bf16(rw_ref[...])))
    lane = lax.broadcasted_iota(jnp.int32, scores.shape, 1)
    biased = jnp.where(lane < n_experts, scores + rb_ref[...], NEG_INF)
    chosen = []
    for _ in range(TOP_K):
        m = jnp.max(biased, axis=-1, keepdims=True)
        pick = jnp.min(jnp.where(biased == m, lane, LANES), axis=-1, keepdims=True)
        sel = lane == pick
        chosen.append((pick, sel))
        biased = jnp.where(sel, NEG_INF, biased)
    mask = functools.reduce(jnp.logical_or, [sel for _, sel in chosen])
    maskf = jnp.where(mask, 1.0, 0.0)
    picked = scores * maskf
    dense_w = picked / jnp.sum(picked, axis=-1, keepdims=True) * ROUTE_SCALE
    row = lax.broadcasted_iota(jnp.int32, (tm, tm), 0)
    col = lax.broadcasted_iota(jnp.int32, (tm, tm), 1)
    before = jnp.where(row > col, 1.0, 0.0).astype(BF16)
    rank = _dot(before, maskf.astype(BF16)) + carry[0:1, :]
    carry[...] = carry[...] + jnp.sum(maskf, axis=0, keepdims=True)
    idx8 = jnp.zeros(scores.shape, jnp.int32)
    rank8 = jnp.zeros(scores.shape, F32)
    wts8 = jnp.zeros(scores.shape, F32)
    for k, (pick, sel) in enumerate(chosen):
        here = lane == k
        idx8 = jnp.where(here, pick, idx8)
        rank8 = jnp.where(here, jnp.sum(jnp.where(sel, rank, 0.0), axis=-1, keepdims=True), rank8)
        wts8 = jnp.where(here, jnp.sum(jnp.where(sel, dense_w, 0.0), axis=-1, keepdims=True), wts8)
    idx_ref[...] = idx8
    rank_ref[...] = rank8.astype(jnp.int32)
    wts_ref[...] = wts8
    cnt_ref[...] = carry[...]


def _dispatch_kernel(tail_ref, has_ref, slot_hbm, x_ref, xs_hbm, slots, zbuf, sem_s, sem_z, sem, *, max_unused):
    i = pl.program_id(0)
    tm = x_ref.shape[0]
    bm = zbuf.shape[0]
    n_exp = has_ref.shape[0] - 1
    n_blocks = xs_hbm.shape[0] // bm

    @pl.when(i == 0)
    def _():
        zbuf[...] = jnp.zeros(zbuf.shape, zbuf.dtype)

        def zero_block(row0):
            return pltpu.make_async_copy(zbuf, xs_hbm.at[pl.ds(row0, bm)], sem_z)

        n_used = has_ref[n_exp]
        for start in (True, False):
            for e in range(n_exp):
                cp = zero_block(pl.multiple_of(tail_ref[e], bm))
                pl.when(has_ref[e] == 1)(cp.start if start else cp.wait)
            for k in range(max_unused):
                cp = zero_block((n_blocks - 1 - k) * bm)
                pl.when(n_blocks - 1 - k >= n_used)(cp.start if start else cp.wait)

    cp = pltpu.make_async_copy(slot_hbm.at[i], slots, sem_s)
    cp.start()
    cp.wait()

    def row_refs(r, j):
        return x_ref.at[pl.ds(r, 1)], xs_hbm.at[pl.ds(slots[r * TOP_K + j], 1)]

    def issue(r, carry):
        for j in range(TOP_K):
            pltpu.async_copy(*row_refs(r, j), sem, priority=j % 2)
        return carry

    lax.fori_loop(0, tm, issue, 0)

    def drain(r, carry):
        for j in range(TOP_K):
            pltpu.make_async_copy(*row_refs(r, j), sem).wait()
        return carry

    lax.fori_loop(0, tm, drain, 0)


def _expert_kernel(be_ref, used_ref, xblk_ref, x_ref, wgu_ref, wd_ref, o_ref):
    del be_ref, xblk_ref
    b = pl.program_id(0)
    f = wd_ref.shape[0]
    half = x_ref.shape[1]

    @pl.when(used_ref[b] == 1)
    def _():
        x_lo, x_hi = _unpack_halves(x_ref[...])
        gu = _dot(x_lo.astype(BF16), wgu_ref[:half, :]) + _dot(x_hi.astype(BF16), wgu_ref[half:, :])
        act = (_silu(gu[:, :f]) * gu[:, f:]).astype(BF16)
        o_ref[...] = _pack_halves(_dot(act, wd_ref[...]))

    @pl.when(used_ref[b] == 0)
    def _():
        o_ref[...] = jnp.zeros(o_ref.shape, jnp.uint32)


def _combine_kernel(slot_hbm, ys_hbm, wts_ref, o_ref, slots, buf, sem_s, sem):
    i = pl.program_id(0)
    tm = o_ref.shape[0]
    half = buf.shape[2]
    cs = pltpu.make_async_copy(slot_hbm.at[i], slots, sem_s)
    cs.start()
    cs.wait()

    def row_refs(r, j):
        return ys_hbm.at[pl.ds(slots[r * TOP_K + j], 1)], buf.at[j, pl.ds(r, 1)]

    def issue(r, carry):
        for j in range(TOP_K):
            pltpu.async_copy(*row_refs(r, j), sem, priority=j % 2)
        return carry

    lax.fori_loop(0, tm, issue, 0)

    def drain(r, carry):
        for j in range(TOP_K):
            pltpu.make_async_copy(*row_refs(r, j), sem).wait()
        return carry

    lax.fori_loop(0, tm, drain, 0)

    acc_lo = acc_hi = None
    for j in range(TOP_K):
        lo, hi = _unpack_halves(buf[j])
        wj = wts_ref[:, j:j + 1]
        acc_lo = wj * lo if acc_lo is None else acc_lo + wj * lo
        acc_hi = wj * hi if acc_hi is None else acc_hi + wj * hi
    o_ref[:, :half] = acc_lo
    o_ref[:, half:] = acc_hi


def _shared_up_kernel(h_ref, w_ref, o_ref):
    gu = _dot(h_ref[...], w_ref[...])
    f = gu.shape[1] // 2
    o_ref[...] = (_silu(gu[:, :f]) * gu[:, f:]).astype(BF16)


def _rope_tables(n_tok, n_rot, lo, width, n_identity):
    rows = n_tok // GRID_W
    row = jnp.repeat(jnp.arange(rows, dtype=F32), GRID_W)
    col = jnp.tile(jnp.arange(GRID_W, dtype=F32), rows)
    half = n_rot // 2
    inv_freq = ROPE_THETA ** (-jnp.arange(0, half, 2, dtype=F32) / half)
    ar, ac = row[:, None] * inv_freq[None, :], col[:, None] * inv_freq[None, :]
    cos = jnp.concatenate([jnp.cos(ar), jnp.cos(ar), jnp.cos(ac), jnp.cos(ac)], axis=1)
    sin = jnp.concatenate([-jnp.sin(ar), jnp.sin(ar), -jnp.sin(ac), jnp.sin(ac)], axis=1)
    cos = jnp.pad(cos, ((0, n_identity), (lo, width - lo - n_rot)), constant_values=1.0)
    cos = cos.at[n_tok:, :].set(1.0)
    sin = jnp.pad(sin, ((0, n_identity), (lo, width - lo - n_rot)))
    return cos, sin


def kernel(x_prompt, x_sample, cache_mla_kv, cache_mla_kr, cache_gqa_k, cache_gqa_v, state_dn, c, c_ctx,
           w_mod, b_mod, norm_mix, norm_ffn, w_in, mla_q_norm, mla_w_q_up, mla_kv_norm, mla_w_kv_up,
           conv_dw_w, conv_dw_b, conv_ln_w, conv_ln_b, gqa_q_norm, gqa_k_norm, dn_conv_w, dn_a_log,
           dn_dt_bias, dn_o_norm, w_branch, w_out, router_w, router_bias, exp_w_gu, exp_w_down,
           sh_w_gu, sh_w_down, final_norm):
    nb, seq, d = x_prompt.shape
    db, dseq, _ = x_sample.shape
    depth = w_mod.shape[0]
    past = cache_mla_kv.shape[2]
    n_exp, _, f2 = exp_w_gu.shape[1:]
    fexp = f2 // 2
    tp, ts = nb * seq, db * dseq
    t = tp + ts
    tm = SEQ_TILE
    assert seq == tm and past == tm and db + 1 <= 8 and tp % (2 * tm) == 0 and dseq % (2 * tm) == 0 and tp % dseq == 0
    npt, spt = tp // tm, dseq // tm
    n_tiles = t // tm

    def row_of(i):
        return jnp.where(i < npt, 0, 1 + (i - npt) // spt)

    def pos_of(i):
        return jnp.where(i < npt, spt, (i - npt) % spt)

    tile_ids = np.arange(n_tiles)
    in_sample = tile_ids >= npt
    edges = jnp.asarray(np.stack([np.where(in_sample, (tile_ids - npt) % spt == 0, True),
                                  np.where(in_sample, (tile_ids - npt) % spt == spt - 1, True)]).astype(np.int32))

    x = jnp.concatenate([x_prompt.reshape(tp, d), x_sample.reshape(ts, d)], axis=0)
    cond8 = jnp.zeros((8, d), F32).at[0].set(c_ctx).at[1:1 + db].set(c)

    cos_m, sin_m = _rope_tables(dseq, MLA_ROPE, MLA_NOPE, HEAD_W, tm)
    cos_k, sin_k = _rope_tables(dseq, MLA_ROPE, 0, LANES, tm)
    cos_g, sin_g = _rope_tables(dseq, GQA_HD, 0, GQA_HD, tm)

    sizes = (Q_LORA, KV_LORA, MLA_ROPE, 2 * CONV_CH, GQA_HEADS * GQA_HD, GQA_KV_HEADS * GQA_HD,
             GQA_KV_HEADS * GQA_HD, DN_HEADS * DN_DK, DN_HEADS * DN_DK, DN_HEADS * DN_DV, DN_HEADS * DN_DV,
             2 * DN_HEADS, 2 * DN_HEADS, N_BRANCH * d)
    names = ("q_c", "kv_c", "k_r", "glu", "g_q", "g_k", "g_v", "d_q", "d_k", "d_v", "d_z", "d_a", "d_b", "gates")
    src = dict(zip(names, zip(np.cumsum((0,) + sizes[:-1]).tolist(), sizes)))
    order = ("gates", "glu", "g_q", "d_q", "d_k", "d_v", "d_z", "kv_c", "g_k", "g_v", "q_c", "k_r", "d_ab")
    col = {}
    off = 0
    src["d_ab"] = (src["d_a"][0], 4 * DN_HEADS)
    for name in order:
        width = max(LANES, src[name][1])
        off = -(-off // width) * width
        col[name] = (off, width)
        off += width
    p_width = -(-off // 512) * 512

    def arrange_w_in(w):
        parts, pos = [], 0
        for name in order:
            a, n = src[name]
            parts.append(jnp.pad(w[:, a:a + n], ((0, 0), (col[name][0] - pos, col[name][1] - n))))
            pos = col[name][0] + col[name][1]
        parts.append(jnp.zeros((w.shape[0], p_width - pos), w.dtype))
        return jnp.concatenate(parts, axis=1).astype(BF16)

    def head_pad(w, n_in, widths, pads):
        h = w.shape[1] // sum(widths)
        w3 = w.reshape(n_in, h, sum(widths))
        parts, a = [], 0
        for wd, pd in zip(widths, pads):
            parts.append(jnp.pad(w3[:, :, a:a + wd], ((0, 0), (0, 0), (0, pd))))
            a += wd
        return jnp.concatenate(parts, axis=2).reshape(n_in, -1).astype(BF16)

    def pcol(name, width=None):
        a, n = col[name]
        width = n if width is None else width
        assert a % width == 0
        return a // width

    w_out_b, w_branch_b = w_out.astype(BF16), w_branch.astype(BF16)
    sh_gu_b, sh_dw_b = sh_w_gu.astype(BF16), sh_w_down.astype(BF16)
    wkv_b = mla_w_kv_up.astype(BF16)

    kv_lats, krs, gks, gvs, states = [], [], [], [], []
    for l in range(depth):
        mod4 = _modulation(cond8, w_mod, b_mod, l).reshape(8, 6, 1, d)
        h = _norm_mod(x, norm_mix, l, mod4, 1, 0, row_of)
        proj = _matmul(h, arrange_w_in(w_in[l])[None], 0, F32, "in_proj")

        wq = head_pad(mla_w_q_up[l], Q_LORA, (MLA_NOPE, MLA_ROPE), (0, HEAD_W - MLA_QK))
        q_mla = pl.pallas_call(
            _mla_q_kernel,
            out_shape=jax.ShapeDtypeStruct((t, MLA_HEADS * HEAD_W), BF16),
            grid=(n_tiles,),
            in_specs=[pl.BlockSpec((tm, Q_LORA), lambda i: (i, pcol("q_c"))),
                      pl.BlockSpec((None, 1, Q_LORA), lambda i: (l, 0, 0)),
                      pl.BlockSpec((Q_LORA, MLA_HEADS * HEAD_W), lambda i: (0, 0)),
                      pl.BlockSpec((tm, HEAD_W), lambda i: (pos_of(i), 0)),
                      pl.BlockSpec((tm, HEAD_W), lambda i: (pos_of(i), 0))],
            out_specs=pl.BlockSpec((tm, MLA_HEADS * HEAD_W), lambda i: (i, 0)),
            compiler_params=_params(("parallel",), 32),
            name="mla_q",
        )(proj, mla_q_norm.reshape(depth, 1, Q_LORA), wq, cos_m, sin_m)
        kv_lat, kv_up, kr_rot = pl.pallas_call(
            _mla_kv_kernel,
            out_shape=[jax.ShapeDtypeStruct((t, KV_LORA), F32),
                       jax.ShapeDtypeStruct((t, MLA_HEADS * HEAD_W), BF16),
                       jax.ShapeDtypeStruct((t, LANES), BF16)],
            grid=(n_tiles,),
            in_specs=[pl.BlockSpec((tm, KV_LORA), lambda i: (i, pcol("kv_c"))),
                      pl.BlockSpec((None, 1, KV_LORA), lambda i: (l, 0, 0)),
                      pl.BlockSpec((None, KV_LORA, MLA_HEADS * HEAD_W), lambda i: (l, 0, 0)),
                      pl.BlockSpec((tm, LANES), lambda i: (i, pcol("k_r"))),
                      pl.BlockSpec((tm, LANES), lambda i: (pos_of(i), 0)),
                      pl.BlockSpec((tm, LANES), lambda i: (pos_of(i), 0))],
            out_specs=[pl.BlockSpec((tm, KV_LORA), lambda i: (i, 0)),
                       pl.BlockSpec((tm, MLA_HEADS * HEAD_W), lambda i: (i, 0)),
                       pl.BlockSpec((tm, LANES), lambda i: (i, 0))],
            compiler_params=_params(("parallel",), 32),
            name="mla_kv",
        )(proj, mla_kv_norm.reshape(depth, 1, KV_LORA), wkv_b, proj, cos_k, sin_k)
        kv_up_c = _matmul(cache_mla_kv[:, l].reshape(db * past, KV_LORA).astype(BF16), wkv_b, l, BF16,
                          "mla_kv_cache")
        kr_c = jnp.pad(cache_mla_kr[:, l].reshape(db * past, MLA_ROPE), ((0, 0), (0, LANES - MLA_ROPE))).astype(BF16)
        o_a = jnp.concatenate([
            _attention(_mla_attn_kernel, q_mla, (kv_up, kr_rot), None, MLA_HEADS * MLA_V, nb, seq, 0, past,
                       "mla_attn_ctx"),
            _attention(_mla_attn_kernel, q_mla, (kv_up, kr_rot), (kv_up_c, kr_c), MLA_HEADS * MLA_V, db, dseq, tp,
                       past, "mla_attn_lat")], axis=0)

        halo_b = 16
        hb = tm // halo_b
        o_b = pl.pallas_call(
            _conformer_kernel,
            out_shape=jax.ShapeDtypeStruct((t, CONV_CH), BF16),
            grid_spec=pltpu.PrefetchScalarGridSpec(
                num_scalar_prefetch=1,
                grid=(n_tiles,),
                in_specs=[pl.BlockSpec((tm, 2 * CONV_CH), lambda i, e: (i, pcol("glu"))),
                          pl.BlockSpec((halo_b, 2 * CONV_CH),
                                       lambda i, e: (jnp.maximum(i * hb - 1, 0), pcol("glu"))),
                          pl.BlockSpec((halo_b, 2 * CONV_CH),
                                       lambda i, e: (jnp.minimum((i + 1) * hb, n_tiles * hb - 1), pcol("glu"))),
                          pl.BlockSpec((None, CONV_K, CONV_CH), lambda i, e: (l, 0, 0)),
                          pl.BlockSpec((None, 1, CONV_CH), lambda i, e: (l, 0, 0)),
                          pl.BlockSpec((None, 1, CONV_CH), lambda i, e: (l, 0, 0)),
                          pl.BlockSpec((None, 1, CONV_CH), lambda i, e: (l, 0, 0))],
                out_specs=pl.BlockSpec((tm, CONV_CH), lambda i, e: (i, 0)),
                scratch_shapes=[pltpu.VMEM((tm + 2 * halo_b, CONV_CH), F32), pltpu.VMEM((tm, CONV_CH), F32)]),
            compiler_params=_params(("parallel",), 32),
            name="conformer_conv",
        )(edges, proj, proj, proj, conv_dw_w, conv_dw_b.reshape(depth, 1, CONV_CH),
          conv_ln_w.reshape(depth, 1, CONV_CH), conv_ln_b.reshape(depth, 1, CONV_CH))

        wq_g, wk_g = GQA_HEADS * GQA_HD, GQA_KV_HEADS * GQA_HD
        q_g, k_g, v_g, k_norm = pl.pallas_call(
            _gqa_prep_kernel,
            out_shape=[jax.ShapeDtypeStruct((t, wq_g), BF16), jax.ShapeDtypeStruct((t, wk_g), BF16),
                       jax.ShapeDtypeStruct((t, wk_g), BF16), jax.ShapeDtypeStruct((t, wk_g), F32)],
            grid=(n_tiles,),
            in_specs=[pl.BlockSpec((tm, wq_g), lambda i: (i, pcol("g_q"))),
                      pl.BlockSpec((tm, wk_g), lambda i: (i, pcol("g_k"))),
                      pl.BlockSpec((tm, wk_g), lambda i: (i, pcol("g_v"))),
                      pl.BlockSpec((None, 1, GQA_HD), lambda i: (l, 0, 0)),
                      pl.BlockSpec((None, 1, GQA_HD), lambda i: (l, 0, 0)),
                      pl.BlockSpec((tm, GQA_HD), lambda i: (pos_of(i), 0)),
                      pl.BlockSpec((tm, GQA_HD), lambda i: (pos_of(i), 0))],
            out_specs=[pl.BlockSpec((tm, wq_g), lambda i: (i, 0)), pl.BlockSpec((tm, wk_g), lambda i: (i, 0)),
                       pl.BlockSpec((tm, wk_g), lambda i: (i, 0)), pl.BlockSpec((tm, wk_g), lambda i: (i, 0))],
            compiler_params=_params(("parallel",), 32),
            name="gqa_prep",
        )(proj, proj, proj, gqa_q_norm.reshape(depth, 1, GQA_HD), gqa_k_norm.reshape(depth, 1, GQA_HD),
          cos_g, sin_g)
        kc_g = cache_gqa_k[:, l].reshape(db * past, wk_g).astype(BF16)
        vc_g = cache_gqa_v[:, l].reshape(db * past, wk_g).astype(BF16)
        o_c = jnp.concatenate([
            _attention(_gqa_attn_kernel, q_g, (k_g, v_g), None, wq_g, nb, seq, 0, past, "gqa_attn_ctx"),
            _attention(_gqa_attn_kernel, q_g, (k_g, v_g), (kc_g, vc_g), wq_g, db, dseq, tp, past,
                       "gqa_attn_lat")], axis=0)

        halo_d = 8
        hd = tm // halo_d
        wdn = DN_HEADS * DN_DK
        prev_map = lambda name: (lambda i, e: (jnp.maximum(i * hd - 1, 0), pcol(name)))
        next_map = lambda name: (lambda i, e: (jnp.minimum((i + 1) * hd, n_tiles * hd - 1), pcol(name)))
        main_specs = [pl.BlockSpec((tm, wdn), (lambda name: (lambda i, e: (i, pcol(name))))(nm))
                      for nm in ("d_q", "d_k", "d_v")]
        prev_specs = [pl.BlockSpec((halo_d, wdn), prev_map(nm)) for nm in ("d_q", "d_k", "d_v")]
        next_specs = [pl.BlockSpec((halo_d, wdn), next_map(nm)) for nm in ("d_q", "d_k", "d_v")]
        pad16 = lambda a: jnp.pad(a.reshape(1, 2 * DN_HEADS), ((0, 0), (0, LANES - 2 * DN_HEADS)))
        qd, kd, vd, gbeta = pl.pallas_call(
            _dn_prep_kernel,
            out_shape=[jax.ShapeDtypeStruct((t, wdn), F32)] * 3 + [jax.ShapeDtypeStruct((t, LANES), F32)],
            grid_spec=pltpu.PrefetchScalarGridSpec(
                num_scalar_prefetch=1,
                grid=(n_tiles,),
                in_specs=main_specs + prev_specs + next_specs + [
                    pl.BlockSpec((tm, LANES), lambda i, e: (i, pcol("d_ab"))),
                    pl.BlockSpec((None, DN_CONV, 3 * wdn), lambda i, e: (l, 0, 0)),
                    pl.BlockSpec((1, LANES), lambda i, e: (0, 0)),
                    pl.BlockSpec((1, LANES), lambda i, e: (0, 0))],
                out_specs=[pl.BlockSpec((tm, wdn), lambda i, e: (i, 0))] * 3
                + [pl.BlockSpec((tm, LANES), lambda i, e: (i, 0))],
                scratch_shapes=[pltpu.VMEM((tm + 2 * halo_d, wdn), F32)]),
            compiler_params=_params(("parallel",), 40),
            name="dn_prep",
        )(edges, *([proj] * 10), dn_conv_w, pad16(dn_a_log[l]), pad16(dn_dt_bias[l]))
        of_p, ob_p, st_p, wgu_l = _dn_scan(qd, kd, vd, gbeta, None, exp_w_gu.reshape(depth, n_exp * d, f2), l,
                                           nb, seq, 0, True, "dn_scan_ctx")
        s0 = state_dn[:, l].reshape(db, 2 * DN_HEADS, DN_DK, DN_DV)
        of_s, ob_s, wdw_l = _dn_scan(qd, kd, vd, gbeta, s0, exp_w_down.reshape(depth, n_exp * fexp, d), l,
                                     db, dseq, tp, False, "dn_scan_lat")
        wgu_l, wdw_l = wgu_l.reshape(n_exp, d, f2), wdw_l.reshape(n_exp, fexp, d)
        o_f = jnp.concatenate([of_p, of_s], axis=0)
        o_bw = jnp.concatenate([ob_p, ob_s], axis=0)
        o_d = pl.pallas_call(
            _dn_post_kernel,
            out_shape=jax.ShapeDtypeStruct((t, wdn), BF16),
            grid=(n_tiles,),
            in_specs=[pl.BlockSpec((tm, wdn), lambda i: (i, 0)), pl.BlockSpec((tm, wdn), lambda i: (i, 0)),
                      pl.BlockSpec((tm, wdn), lambda i: (i, pcol("d_z"))),
                      pl.BlockSpec((None, 1, DN_DV), lambda i: (l, 0, 0))],
            out_specs=pl.BlockSpec((tm, wdn), lambda i: (i, 0)),
            compiler_params=_params(("parallel",), 32),
            name="dn_post",
        )(o_f, o_bw, proj, dn_o_norm.reshape(depth, 1, DN_DV))

        tmm, tnm = _pick(t, 1024), _pick(d, 512)
        g0 = col["gates"][0] // tnm
        wb = w_branch_b
        branch_specs = [pl.BlockSpec((tmm, BRANCH_W), lambda i, j: (i, 0))] * N_BRANCH
        w_specs = [pl.BlockSpec((None, None, BRANCH_W, tnm), (lambda b: (lambda i, j: (l, b, 0, j)))(b))
                   for b in range(N_BRANCH)]
        gate_specs = [pl.BlockSpec((tmm, tnm), (lambda b: (lambda i, j: (i, g0 + b * (d // tnm) + j)))(b))
                      for b in range(N_BRANCH)]
        merged = pl.pallas_call(
            _merge_kernel,
            out_shape=jax.ShapeDtypeStruct((t, d), BF16),
            grid=(t // tmm, d // tnm),
            in_specs=branch_specs + w_specs + gate_specs,
            out_specs=pl.BlockSpec((tmm, tnm), lambda i, j: (i, j)),
            compiler_params=_params(("parallel", "parallel"), 56),
            name="branch_merge",
        )(o_a, o_b, o_c, o_d, wb, wb, wb, wb, proj, proj, proj, proj)
        x = _matmul_residual(merged, w_out_b, l, x, mod4, 2, row_of, None, "out_proj")

        n_pad = LANES - n_exp
        rw = jnp.pad(router_w[l], ((0, 0), (0, n_pad)))
        rb = jnp.pad(router_bias[l].reshape(1, n_exp), ((0, 0), (0, n_pad)))
        half = d // 2
        h2, h2_packed, idx8, rank8, wts8, counts = pl.pallas_call(
            functools.partial(_ffn_norm_router_kernel, n_experts=n_exp),
            out_shape=[jax.ShapeDtypeStruct((t, d), BF16), jax.ShapeDtypeStruct((t, half), jnp.uint32),
                       jax.ShapeDtypeStruct((t, LANES), jnp.int32),
                       jax.ShapeDtypeStruct((t, LANES), jnp.int32), jax.ShapeDtypeStruct((t, LANES), F32),
                       jax.ShapeDtypeStruct((8, LANES), F32)],
            grid=(n_tiles,),
            in_specs=[pl.BlockSpec((tm, d), lambda i: (i, 0)),
                      pl.BlockSpec((None, 1, d), lambda i: (l, 0, 0)),
                      pl.BlockSpec((None, None, 1, d), lambda i: (row_of(i), 4, 0, 0)),
                      pl.BlockSpec((None, None, 1, d), lambda i: (row_of(i), 3, 0, 0)),
                      pl.BlockSpec((d, LANES), lambda i: (0, 0)),
                      pl.BlockSpec((1, LANES), lambda i: (0, 0))],
            out_specs=[pl.BlockSpec((tm, d), lambda i: (i, 0)), pl.BlockSpec((tm, half), lambda i: (i, 0)),
                       pl.BlockSpec((tm, LANES), lambda i: (i, 0)),
                       pl.BlockSpec((tm, LANES), lambda i: (i, 0)), pl.BlockSpec((tm, LANES), lambda i: (i, 0)),
                       pl.BlockSpec((8, LANES), lambda i: (0, 0))],
            scratch_shapes=[pltpu.VMEM((8, LANES), F32)],
            compiler_params=_params(("arbitrary",), 48),
            name="ffn_norm_router",
        )(x, norm_ffn.reshape(depth, 1, d), mod4, mod4, rw, rb)

        bm = EXPERT_BLOCK
        n_blocks = (t * TOP_K + n_exp * (bm - 1) + bm - 1) // bm
        cnt = counts[0, :n_exp].astype(jnp.int32)
        padded = (cnt + bm - 1) // bm * bm
        pad_end = jnp.cumsum(padded)
        pad_start = pad_end - padded
        slot8 = (pad_start[idx8[:, :TOP_K]] + rank8[:, :TOP_K]).astype(jnp.int32)
        blk = jnp.arange(n_blocks, dtype=jnp.int32)
        block_e = jnp.minimum(jnp.sum((pad_end[None, :] <= blk[:, None] * bm).astype(jnp.int32), axis=1),
                              n_exp - 1).astype(jnp.int32)
        n_used = pad_end[-1] // bm
        used = (blk < n_used).astype(jnp.int32)
        x_blk = jnp.minimum(blk, n_used - 1).astype(jnp.int32)
        tail = (pad_end - bm).astype(jnp.int32)
        has = jnp.concatenate([(padded > 0).astype(jnp.int32), n_used.reshape(1).astype(jnp.int32)])
        max_unused = n_blocks - (t * TOP_K + bm - 1) // bm

        n_slots = n_blocks * bm
        xs = pl.pallas_call(
            functools.partial(_dispatch_kernel, max_unused=max_unused),
            out_shape=jax.ShapeDtypeStruct((n_slots, half), jnp.uint32),
            grid_spec=pltpu.PrefetchScalarGridSpec(
                num_scalar_prefetch=2,
                grid=(n_tiles,),
                in_specs=[pl.BlockSpec(memory_space=pl.ANY),
                          pl.BlockSpec((tm, half), lambda i, tl, hs: (i, 0))],
                out_specs=pl.BlockSpec(memory_space=pl.ANY),
                scratch_shapes=[pltpu.SMEM((tm * TOP_K,), jnp.int32), pltpu.VMEM((bm, half), jnp.uint32),
                                pltpu.SemaphoreType.DMA, pltpu.SemaphoreType.DMA, pltpu.SemaphoreType.DMA]),
            compiler_params=_params(("arbitrary",), 32),
            name="moe_dispatch",
        )(tail, has, slot8.reshape(n_tiles, tm * TOP_K), h2_packed)

        ys = pl.pallas_call(
            _expert_kernel,
            out_shape=jax.ShapeDtypeStruct((n_slots, half), jnp.uint32),
            grid_spec=pltpu.PrefetchScalarGridSpec(
                num_scalar_prefetch=3,
                grid=(n_blocks,),
                in_specs=[pl.BlockSpec((bm, half), lambda b, be, us, xb: (xb[b], 0)),
                          pl.BlockSpec((None, d, f2), lambda b, be, us, xb: (be[b], 0, 0)),
                          pl.BlockSpec((None, fexp, d), lambda b, be, us, xb: (be[b], 0, 0))],
                out_specs=pl.BlockSpec((bm, half), lambda b, be, us, xb: (b, 0))),
            compiler_params=_params(("arbitrary",), 48),
            name="moe_experts",
        )(block_e, used, x_blk, xs, wgu_l, wdw_l)

        tmc = 128
        y_routed = pl.pallas_call(
            _combine_kernel,
            out_shape=jax.ShapeDtypeStruct((t, d), F32),
            grid=(t // tmc,),
            in_specs=[pl.BlockSpec(memory_space=pl.ANY), pl.BlockSpec(memory_space=pl.ANY),
                      pl.BlockSpec((tmc, LANES), lambda i: (i, 0))],
            out_specs=pl.BlockSpec((tmc, d), lambda i: (i, 0)),
            scratch_shapes=[pltpu.SMEM((tmc * TOP_K,), jnp.int32),
                            pltpu.VMEM((TOP_K, tmc, half), jnp.uint32),
                            pltpu.SemaphoreType.DMA, pltpu.SemaphoreType.DMA],
            compiler_params=_params(("arbitrary",), 40),
            name="moe_combine",
        )(slot8.reshape(t // tmc, tmc * TOP_K), ys, wts8)

        tms = _pick(t, 512)
        f2s = sh_w_gu.shape[2]
        act_sh = pl.pallas_call(
            _shared_up_kernel,
            out_shape=jax.ShapeDtypeStruct((t, f2s // 2), BF16),
            grid=(t // tms,),
            in_specs=[pl.BlockSpec((tms, d), lambda i: (i, 0)),
                      pl.BlockSpec((None, d, f2s), lambda i: (l, 0, 0))],
            out_specs=pl.BlockSpec((tms, f2s // 2), lambda i: (i, 0)),
            compiler_params=_params(("parallel",), 40),
            name="shared_up",
        )(h2, sh_gu_b)
        x = _matmul_residual(act_sh, sh_dw_b, l, x, mod4, 5, row_of, y_routed, "ffn_out")

        kv_lats.append(kv_lat[:tp].reshape(nb, seq, KV_LORA))
        a, n = col["k_r"][0], MLA_ROPE
        krs.append(proj[:tp, a:a + n].reshape(nb, seq, MLA_ROPE))
        gks.append(k_norm[:tp].reshape(nb, seq, GQA_KV_HEADS, GQA_HD))
        a, n = col["g_v"]
        gvs.append(proj[:tp, a:a + n].reshape(nb, seq, GQA_KV_HEADS, GQA_HD))
        states.append(st_p.reshape(nb, 2, DN_HEADS, DN_DK, DN_DV))

    y = _final_norm(x, final_norm)
    return (y[:tp].reshape(nb, seq, d), y[tp:].reshape(db, dseq, d),
            jnp.stack(kv_lats, axis=1), jnp.stack(krs, axis=1), jnp.stack(gks, axis=1),
            jnp.stack(gvs, axis=1), jnp.stack(states, axis=1))
```

```python
import functools
import math

import numpy as np
import jax
import jax.numpy as jnp
from jax import lax
from jax.experimental import pallas as pl
from jax.experimental.pallas import tpu as pltpu

F32 = jnp.float32
BF16 = jnp.bfloat16
EPS = 1e-6
NEG_INF = float("-inf")

GRID_W = 64
ROPE_THETA = 10000.0
MLA_HEADS, MLA_NOPE, MLA_ROPE, MLA_V = 8, 128, 64, 128
MLA_QK = MLA_NOPE + MLA_ROPE
Q_LORA, KV_LORA = 768, 512
CONV_CH, CONV_K = 1024, 31
GQA_HEADS, GQA_KV_HEADS, GQA_HD = 8, 2, 128
DN_HEADS, DN_DK, DN_DV, DN_CONV, DN_CHUNK = 8, 128, 128, 5, 64
N_BRANCH, BRANCH_W = 4, 1024
TOP_K = 8
ROUTE_SCALE = 2.5
LANES = 128
SEQ_TILE = 256
EXPERT_BLOCK = 256
HEAD_W = 2 * LANES


def _params(sem, vmem_mb):
    return pltpu.CompilerParams(dimension_semantics=sem, vmem_limit_bytes=vmem_mb << 20)


def _pick(n, pref):
    t = min(pref, n)
    while n % t:
        t //= 2
    return t


def _dot(a, b, precision=None):
    return jnp.dot(a, b, preferred_element_type=F32, precision=precision)


def _dot_nt(a, b):
    return lax.dot_general(a, b, (((1,), (1,)), ((), ())), preferred_element_type=F32)


def _dot_tn(a, b):
    return lax.dot_general(a, b, (((0,), (0,)), ((), ())), preferred_element_type=F32)


def _silu(x):
    return x * jax.nn.sigmoid(x)


def _rms(x, gain):
    return x * lax.rsqrt(jnp.mean(x * x, axis=-1, keepdims=True) + EPS) * gain


def _rope(x, cos, sin, quarter):
    n = x.shape[-1]
    lane = lax.broadcasted_iota(jnp.int32, x.shape, x.ndim - 1)
    first = (lane % (2 * quarter)) < quarter
    sw = jnp.where(first, pltpu.roll(x, n - quarter, x.ndim - 1), pltpu.roll(x, quarter, x.ndim - 1))
    return x * cos + sw * sin


def _mod_kernel(c_ref, w_ref, b_ref, o_ref):
    a = _silu(c_ref[...]).astype(BF16)
    o_ref[...] = _dot(a, w_ref[...].astype(BF16)) + b_ref[...]


def _modulation(cond8, w_mod, b_mod, l):
    d, n = w_mod.shape[1], w_mod.shape[2]
    tn = _pick(n, 512)
    return pl.pallas_call(
        _mod_kernel,
        out_shape=jax.ShapeDtypeStruct((8, n), F32),
        grid=(n // tn,),
        in_specs=[pl.BlockSpec((8, d), lambda j: (0, 0)),
                  pl.BlockSpec((None, d, tn), lambda j: (l, 0, j)),
                  pl.BlockSpec((None, 1, tn), lambda j: (l, 0, j))],
        out_specs=pl.BlockSpec((8, tn), lambda j: (0, j)),
        compiler_params=_params(("arbitrary",), 40),
        name="modulation",
    )(cond8, w_mod, b_mod.reshape(b_mod.shape[0], 1, n))


def _norm_mod_kernel(x_ref, g_ref, sc_ref, sh_ref, o_ref):
    y = _rms(x_ref[...], g_ref[...])
    o_ref[...] = (y * (1.0 + sc_ref[...]) + sh_ref[...]).astype(o_ref.dtype)


def _norm_mod(x, gain, l, mod4, k_scale, k_shift, row_of):
    t, d = x.shape
    tm = SEQ_TILE
    return pl.pallas_call(
        _norm_mod_kernel,
        out_shape=jax.ShapeDtypeStruct((t, d), BF16),
        grid=(t // tm,),
        in_specs=[pl.BlockSpec((tm, d), lambda i: (i, 0)),
                  pl.BlockSpec((None, 1, d), lambda i: (l, 0, 0)),
                  pl.BlockSpec((None, None, 1, d), lambda i: (row_of(i), k_scale, 0, 0)),
                  pl.BlockSpec((None, None, 1, d), lambda i: (row_of(i), k_shift, 0, 0))],
        out_specs=pl.BlockSpec((tm, d), lambda i: (i, 0)),
        compiler_params=_params(("parallel",), 32),
        name="norm_mod",
    )(x, gain.reshape(gain.shape[0], 1, d), mod4, mod4)


def _final_norm_kernel(x_ref, g_ref, o_ref):
    o_ref[...] = _rms(x_ref[...], g_ref[...])


def _final_norm(x, gain):
    t, d = x.shape
    tm = SEQ_TILE
    return pl.pallas_call(
        _final_norm_kernel,
        out_shape=jax.ShapeDtypeStruct((t, d), F32),
        grid=(t // tm,),
        in_specs=[pl.BlockSpec((tm, d), lambda i: (i, 0)),
                  pl.BlockSpec((1, d), lambda i: (0, 0))],
        out_specs=pl.BlockSpec((tm, d), lambda i: (i, 0)),
        compiler_params=_params(("parallel",), 32),
        name="final_norm",
    )(x, gain.reshape(1, d))


def _mm_kernel(a_ref, w_ref, o_ref):
    o_ref[...] = _dot(a_ref[...], w_ref[...]).astype(o_ref.dtype)


def _matmul(a, w, l, out_dtype, name, tm_pref=1024, tn_pref=512):
    m, k = a.shape
    n = w.shape[2]
    tm, tn = _pick(m, tm_pref), _pick(n, tn_pref)
    return pl.pallas_call(
        _mm_kernel,
        out_shape=jax.ShapeDtypeStruct((m, n), out_dtype),
        grid=(n // tn, m // tm),
        in_specs=[pl.BlockSpec((tm, k), lambda j, i: (i, 0)),
                  pl.BlockSpec((None, k, tn), lambda j, i: (l, 0, j))],
        out_specs=pl.BlockSpec((tm, tn), lambda j, i: (i, j)),
        compiler_params=_params(("parallel", "parallel"), 48),
        name=name,
    )(a, w)


def _mm_res_kernel(a_ref, w_ref, res_ref, gate_ref, *rest):
    o_ref = rest[-1]
    y = _dot(a_ref[...], w_ref[...])
    if len(rest) == 2:
        y = y + rest[0][...]
    o_ref[...] = res_ref[...] + gate_ref[...] * y


def _matmul_residual(a, w, l, res, mod4, k_gate, row_of, add, name):
    m, k = a.shape
    n = w.shape[2]
    tm, tn = SEQ_TILE * 2, _pick(n, 1024)
    row2 = lambda i: row_of(2 * i)
    in_specs = [pl.BlockSpec((tm, k), lambda j, i: (i, 0)),
                pl.BlockSpec((None, k, tn), lambda j, i: (l, 0, j)),
                pl.BlockSpec((tm, tn), lambda j, i: (i, j)),
                pl.BlockSpec((None, None, 1, tn), lambda j, i: (row2(i), k_gate, 0, j))]
    args = [a, w, res, mod4]
    if add is not None:
        in_specs.append(pl.BlockSpec((tm, tn), lambda j, i: (i, j)))
        args.append(add)
    return pl.pallas_call(
        _mm_res_kernel,
        out_shape=jax.ShapeDtypeStruct((m, n), F32),
        grid=(n // tn, m // tm),
        in_specs=in_specs,
        out_specs=pl.BlockSpec((tm, tn), lambda j, i: (i, j)),
        compiler_params=_params(("parallel", "parallel"), 48),
        name=name,
    )(*args)


def _mla_q_kernel(qc_ref, g_ref, w_ref, cos_ref, sin_ref, o_ref):
    xn = _rms(qc_ref[...], g_ref[...]).astype(BF16)
    q = _dot(xn, w_ref[...])
    cos, sin = cos_ref[...], sin_ref[...]
    for h in range(MLA_HEADS):
        qh = q[:, h * HEAD_W:(h + 1) * HEAD_W]
        o_ref[:, h * HEAD_W:(h + 1) * HEAD_W] = _rope(qh, cos, sin, MLA_ROPE // 4).astype(BF16)


def _mla_kv_kernel(kvc_ref, g_ref, w_ref, kr_ref, cos_ref, sin_ref, lat_ref, up_ref, kro_ref):
    lat = _rms(kvc_ref[...], g_ref[...])
    lat_ref[...] = lat
    up_ref[...] = _dot(lat.astype(BF16), w_ref[...]).astype(BF16)
    kro_ref[...] = _rope(kr_ref[...], cos_ref[...], sin_ref[...], MLA_ROPE // 4).astype(BF16)


def _softmax_pv(s_list, v_list):
    m = functools.reduce(jnp.maximum, [jnp.max(s, axis=-1, keepdims=True) for s in s_list])
    p_list = [jnp.exp(s - m) for s in s_list]
    den = functools.reduce(jnp.add, [jnp.sum(p, axis=-1, keepdims=True) for p in p_list])
    acc = functools.reduce(jnp.add, [_dot(p.astype(BF16), v) for p, v in zip(p_list, v_list)])
    return acc / den


def _mla_attn_kernel(*refs, has_cache):
    if has_cache:
        q_ref, kv_ref, kr_ref, kvc_ref, krc_ref, o_ref = refs
    else:
        q_ref, kv_ref, kr_ref, o_ref = refs
    scale = MLA_QK ** -0.5
    for h in range(MLA_HEADS):
        lo = h * HEAD_W
        qn = q_ref[:, lo:lo + LANES]
        qr = q_ref[:, lo + LANES:lo + HEAD_W]
        parts = [(kv_ref, kr_ref)] + ([(kvc_ref, krc_ref)] if has_cache else [])
        s_list, v_list = [], []
        for kvr, krr in parts:
            s = _dot_nt(qn, kvr[:, lo:lo + LANES]) + _dot_nt(qr, krr[...])
            s_list.append(s * scale)
            v_list.append(kvr[:, lo + LANES:lo + HEAD_W])
        o_ref[:, h * MLA_V:(h + 1) * MLA_V] = _softmax_pv(s_list, v_list).astype(BF16)


def _gqa_attn_kernel(*refs, has_cache):
    if has_cache:
        q_ref, k_ref, v_ref, kc_ref, vc_ref, o_ref = refs
    else:
        q_ref, k_ref, v_ref, o_ref = refs
    scale = GQA_HD ** -0.5
    grp = GQA_HEADS // GQA_KV_HEADS
    for h in range(GQA_HEADS):
        lo = (h // grp) * GQA_HD
        q = q_ref[:, h * GQA_HD:(h + 1) * GQA_HD]
        parts = [(k_ref, v_ref)] + ([(kc_ref, vc_ref)] if has_cache else [])
        s_list = [_dot_nt(q, kr[:, lo:lo + GQA_HD]) * scale for kr, _ in parts]
        v_list = [vr[:, lo:lo + GQA_HD] for _, vr in parts]
        o_ref[:, h * GQA_HD:(h + 1) * GQA_HD] = _softmax_pv(s_list, v_list).astype(BF16)


def _attention(kernel, q, kv_list, cache_list, out_w, n_seq, seq_len, row0, past, name):
    tq = SEQ_TILE
    qb = seq_len // tq
    blk0 = row0 // tq
    seq0 = row0 // seq_len
    has_cache = cache_list is not None
    in_specs = [pl.BlockSpec((tq, q.shape[1]), lambda b, i: (blk0 + b * qb + i, 0))]
    in_specs += [pl.BlockSpec((seq_len, a.shape[1]), lambda b, i: (seq0 + b, 0)) for a in kv_list]
    args = [q] + list(kv_list)
    if has_cache:
        in_specs += [pl.BlockSpec((past, a.shape[1]), lambda b, i: (b, 0)) for a in cache_list]
        args += list(cache_list)
    return pl.pallas_call(
        functools.partial(kernel, has_cache=has_cache),
        out_shape=jax.ShapeDtypeStruct((n_seq * seq_len, out_w), BF16),
        grid=(n_seq, qb),
        in_specs=in_specs,
        out_specs=pl.BlockSpec((tq, out_w), lambda b, i: (b * qb + i, 0)),
        compiler_params=_params(("parallel", "parallel"), 48),
        name=name,
    )(*args)


def _gqa_prep_kernel(q_ref, k_ref, v_ref, gq_ref, gk_ref, cos_ref, sin_ref, qo_ref, ko_ref, vo_ref, kn_ref):
    cos, sin = cos_ref[...], sin_ref[...]
    for h in range(GQA_HEADS):
        sl = slice(h * GQA_HD, (h + 1) * GQA_HD)
        qo_ref[:, sl] = _rope(_rms(q_ref[:, sl], gq_ref[...]), cos, sin, GQA_HD // 4).astype(BF16)
    for h in range(GQA_KV_HEADS):
        sl = slice(h * GQA_HD, (h + 1) * GQA_HD)
        kn = _rms(k_ref[:, sl], gk_ref[...])
        kn_ref[:, sl] = kn
        ko_ref[:, sl] = _rope(kn, cos, sin, GQA_HD // 4).astype(BF16)
    vo_ref[...] = v_ref[...].astype(BF16)


def _fill_halo_buffer(buf, main, prev, nxt, first, last, halo):
    tm = main.shape[0]
    buf[pl.ds(halo, tm), :] = main
    buf[pl.ds(0, halo), :] = jnp.where(first, 0.0, prev)
    buf[pl.ds(halo + tm, halo), :] = jnp.where(last, 0.0, nxt)


def _depthwise(buf, w_ref, width, halo, tm, lo, w_lo, n):
    pad = width // 2
    acc = jnp.zeros((tm, n), F32)
    for k in range(width):
        acc = acc + w_ref[k:k + 1, w_lo:w_lo + n] * buf[pl.ds(halo - pad + k, tm), lo:lo + n]
    return acc


def _glu(x):
    return x[:, :CONV_CH] * jax.nn.sigmoid(x[:, CONV_CH:])


def _conformer_kernel(edge_ref, x_ref, xp_ref, xn_ref, w_ref, b_ref, lw_ref, lb_ref, o_ref, buf, ybuf):
    i = pl.program_id(0)
    first, last = edge_ref[0, i] == 1, edge_ref[1, i] == 1
    halo = xp_ref.shape[0]
    tm = x_ref.shape[0]
    _fill_halo_buffer(buf, _glu(x_ref[...]), _glu(xp_ref[...]), _glu(xn_ref[...]), first, last, halo)
    for c in range(CONV_CH // LANES):
        lo = c * LANES
        ybuf[:, lo:lo + LANES] = _depthwise(buf, w_ref, CONV_K, halo, tm, lo, lo, LANES) + b_ref[:, lo:lo + LANES]
    y = ybuf[...]
    yc = y - jnp.mean(y, axis=-1, keepdims=True)
    yn = yc * lax.rsqrt(jnp.mean(yc * yc, axis=-1, keepdims=True) + EPS) * lw_ref[...] + lb_ref[...]
    o_ref[...] = _silu(yn).astype(BF16)


def _dn_prep_kernel(edge_ref, q_ref, k_ref, v_ref, qp_ref, kp_ref, vp_ref, qn_ref, kn_ref, vn_ref,
                    ab_ref, w_ref, alog_ref, dtb_ref, qo_ref, ko_ref, vo_ref, gb_ref, buf):
    i = pl.program_id(0)
    first, last = edge_ref[0, i] == 1, edge_ref[1, i] == 1
    halo = qp_ref.shape[0]
    tm = q_ref.shape[0]
    width = DN_HEADS * DN_DK
    groups = ((q_ref, qp_ref, qn_ref, qo_ref, DN_DK ** -0.5), (k_ref, kp_ref, kn_ref, ko_ref, 1.0),
              (v_ref, vp_ref, vn_ref, vo_ref, None))
    for s, (x_ref, xp_ref, xn_ref, o_ref, norm_scale) in enumerate(groups):
        _fill_halo_buffer(buf, x_ref[...], xp_ref[...], xn_ref[...], first, last, halo)
        for h in range(DN_HEADS):
            lo = h * DN_DK
            y = _silu(_depthwise(buf, w_ref, DN_CONV, halo, tm, lo, s * width + lo, DN_DK))
            if norm_scale is not None:
                y = y * (lax.rsqrt(jnp.sum(y * y, axis=-1, keepdims=True) + EPS) * norm_scale)
            o_ref[:, lo:lo + DN_DK] = y
    x = ab_ref[...]
    lane = lax.broadcasted_iota(jnp.int32, x.shape, 1)
    z = x + dtb_ref[...]
    softplus = jnp.maximum(z, 0.0) + jnp.log(1.0 + jnp.exp(-jnp.abs(z)))
    g = -jnp.exp(alog_ref[...]) * softplus
    gb_ref[...] = jnp.where(lane < 2 * DN_HEADS, g, jax.nn.sigmoid(x))


def _split_bf16(x):
    hi = x.astype(BF16)
    return hi, (x - hi.astype(F32)).astype(BF16)


def _dot_split(a_hi, a_lo, b_hi, b_lo):
    m = a_hi.shape[0]
    top = _dot(jnp.concatenate([a_hi, a_lo], axis=0), b_hi)
    return top[:m] + top[m:] + _dot(a_hi, b_lo)


def _dn_scan_kernel(*refs, n_chunks, has_init, want_state):
    names = ["qf", "kf", "vf", "gf", "qb", "kb", "vb", "gb"]
    refs = list(refs)
    ins = {n: refs.pop(0) for n in names}
    s0_ref = refs.pop(0) if has_init else None
    wsrc_ref = refs.pop(0)
    of_ref, ob_ref = refs.pop(0), refs.pop(0)
    sout_ref = refs.pop(0) if want_state else None
    wdst_ref = refs.pop(0)
    state = refs.pop(0)
    c = pl.program_id(1)
    n = DN_CHUNK

    wdst_ref[...] = wsrc_ref[...].astype(BF16)

    @pl.when(c == 0)
    def _():
        if has_init:
            state[...] = s0_ref[...]
        else:
            state[...] = jnp.zeros(state.shape, F32)

    row = lax.broadcasted_iota(jnp.int32, (n, 2 * n), 0)
    lane = lax.broadcasted_iota(jnp.int32, (n, 2 * n), 1)
    col = lane % n
    left = lane < n
    eye = jnp.where(row == col, 1.0, 0.0)
    row1 = lax.broadcasted_iota(jnp.int32, (n, n), 0)
    col1 = lax.broadcasted_iota(jnp.int32, (n, n), 1)

    def block_diag(z):
        zero = jnp.zeros(z.shape, z.dtype)
        return jnp.concatenate([jnp.where(left, z, zero), jnp.where(left, zero, z)], axis=0)

    pairs = []
    for d, (qn, kn, vn, gn, o_ref) in enumerate((("qf", "kf", "vf", "gf", of_ref), ("qb", "kb", "vb", "gb", ob_ref))):
        incl = (row >= col) if d == 0 else (row <= col)
        strict = (row > col) if d == 0 else (row < col)
        incl1 = (row1 >= col1) if d == 0 else (row1 <= col1)
        gb = ins[gn][...]
        gcum = _dot(incl1.astype(F32), gb, lax.Precision.HIGHEST)
        gcum_t = gcum.T
        gtot = jnp.sum(gb, axis=0, keepdims=True)
        for h in range(0, DN_HEADS, 2):
            heads = []
            for hh in (h, h + 1):
                j = d * DN_HEADS + hh
                sl = slice(hh * DN_DK, (hh + 1) * DN_DK)
                k = ins[kn][:, sl]
                gc = gcum[:, j:j + 1]
                beta = gb[:, 2 * DN_HEADS + j:2 * DN_HEADS + j + 1]
                heads.append(dict(j=j, sl=sl, q=ins[qn][:, sl], k=k, v=ins[vn][:, sl], gc=gc, beta=beta,
                                  g_last=gtot[:, j:j + 1], kb=k * beta, eg=jnp.exp(gc)))
            j0 = d * DN_HEADS + h
            gc2 = jnp.where(left, heads[0]["gc"], heads[1]["gc"])
            gr2 = jnp.concatenate([gcum_t[j0:j0 + 1, :], gcum_t[j0 + 1:j0 + 2, :]], axis=1)
            decay = jnp.exp(jnp.where(incl, gc2 - gr2, NEG_INF))
            pairs.append(dict(heads=heads, o_ref=o_ref, decay=decay, strict=strict))

    for p in pairs:
        a = jnp.concatenate([_dot_nt(jnp.concatenate([hd["kb"], hd["q"]], axis=0), hd["k"]) for hd in p["heads"]],
                            axis=1)
        p["P"] = -jnp.where(p["strict"], a[:n] * p["decay"], 0.0)
        p["qk"] = a[n:] * p["decay"]
        p["X"] = eye + p["P"]
    for p in pairs:
        hi, lo = _split_bf16(p["P"])
        p["P"] = _dot_split(hi, lo, block_diag(hi), block_diag(lo))
    steps = int(math.log2(n)) - 1
    for it in range(steps):
        for p in pairs:
            last = it == steps - 1
            y = p["X"] if last else jnp.concatenate([p["X"], p["P"]], axis=0)
            y_hi, y_lo = _split_bf16(y)
            p_hi, p_lo = _split_bf16(p["P"]) if last else (y_hi[n:], y_lo[n:])
            r = _dot_split(y_hi, y_lo, block_diag(p_hi), block_diag(p_lo))
            p["X"] = p["X"] + r[:n]
            if not last:
                p["P"] = r[n:]
    for p in pairs:
        t_hi, t_lo = _split_bf16(p["X"])
        rhs = [jnp.concatenate([hd["v"] * hd["beta"], hd["kb"] * hd["eg"]], axis=1) for hd in p["heads"]]
        zero = jnp.zeros(rhs[0].shape, F32)
        r_hi, r_lo = _split_bf16(jnp.concatenate([jnp.concatenate([rhs[0], zero], axis=1),
                                                  jnp.concatenate([zero, rhs[1]], axis=1)], axis=0))
        p["sol"] = _dot_split(t_hi, t_lo, r_hi, r_lo)
    for p in pairs:
        for hd in p["heads"]:
            hd["s"] = state[hd["j"]]
    for p in pairs:
        deltas = []
        for i, hd in enumerate(p["heads"]):
            u = p["sol"][:, (2 * i) * DN_DV:(2 * i + 1) * DN_DV]
            w = p["sol"][:, (2 * i + 1) * DN_DV:(2 * i + 2) * DN_DV]
            b = _dot(jnp.concatenate([w, hd["q"] * hd["eg"]], axis=0), hd["s"])
            hd["qs"] = b[n:]
            deltas.append(u - b[:n])
            hd["s_new"] = hd["s"] * jnp.exp(hd["g_last"]) + _dot_tn(hd["k"] * jnp.exp(hd["g_last"] - hd["gc"]),
                                                                      deltas[i])
        zero = jnp.zeros(deltas[0].shape, F32)
        dd = jnp.concatenate([jnp.concatenate([deltas[0], zero], axis=1),
                              jnp.concatenate([zero, deltas[1]], axis=1)], axis=0)
        p["qkd"] = _dot(p["qk"], dd)
    for p in pairs:
        for i, hd in enumerate(p["heads"]):
            p["o_ref"][:, hd["sl"]] = hd["qs"] + p["qkd"][:, i * DN_DV:(i + 1) * DN_DV]
            state[hd["j"]] = hd["s_new"]

    if want_state:
        @pl.when(c == n_chunks - 1)
        def _():
            sout_ref[...] = state[...]


def _dn_scan(q, k, v, gb, s0, w_src, l, n_seq, seq_len, row0, want_state, name):
    n_chunks = seq_len // DN_CHUNK
    blk0 = row0 // DN_CHUNK
    w = DN_HEADS * DN_DK
    nst = 2 * DN_HEADS
    n_steps = n_seq * n_chunks
    w_rows, w_cols = w_src.shape[1], w_src.shape[2]
    slab = w_rows // n_steps
    assert slab * n_steps == w_rows and slab % 16 == 0
    fwd = lambda b, c: (blk0 + b * n_chunks + c, 0)
    bwd = lambda b, c: (blk0 + b * n_chunks + n_chunks - 1 - c, 0)
    ofwd = lambda b, c: (b * n_chunks + c, 0)
    obwd = lambda b, c: (b * n_chunks + n_chunks - 1 - c, 0)
    in_specs, args = [], []
    for imap in (fwd, bwd):
        in_specs += [pl.BlockSpec((DN_CHUNK, w), imap)] * 3 + [pl.BlockSpec((DN_CHUNK, LANES), imap)]
        args += [q, k, v, gb]
    if s0 is not None:
        in_specs.append(pl.BlockSpec((None, nst, DN_DK, DN_DV), lambda b, c: (b, 0, 0, 0)))
        args.append(s0)
    in_specs.append(pl.BlockSpec((None, slab, w_cols), lambda b, c: (l, b * n_chunks + c, 0)))
    args.append(w_src)
    out_shape = [jax.ShapeDtypeStruct((n_seq * seq_len, w), F32)] * 2
    out_specs = [pl.BlockSpec((DN_CHUNK, w), ofwd), pl.BlockSpec((DN_CHUNK, w), obwd)]
    if want_state:
        out_shape.append(jax.ShapeDtypeStruct((n_seq, nst, DN_DK, DN_DV), F32))
        out_specs.append(pl.BlockSpec((None, nst, DN_DK, DN_DV), lambda b, c: (b, 0, 0, 0)))
    out_shape.append(jax.ShapeDtypeStruct((w_rows, w_cols), BF16))
    out_specs.append(pl.BlockSpec((slab, w_cols), lambda b, c: (b * n_chunks + c, 0)))
    return pl.pallas_call(
        functools.partial(_dn_scan_kernel, n_chunks=n_chunks, has_init=s0 is not None, want_state=want_state),
        out_shape=out_shape,
        grid=(n_seq, n_chunks),
        in_specs=in_specs,
        out_specs=out_specs,
        scratch_shapes=[pltpu.VMEM((nst, DN_DK, DN_DV), F32)],
        compiler_params=_params(("parallel", "arbitrary"), 48),
        name=name,
    )(*args)


def _dn_post_kernel(of_ref, ob_ref, z_ref, g_ref, o_ref):
    for h in range(DN_HEADS):
        sl = slice(h * DN_DV, (h + 1) * DN_DV)
        o = _rms(of_ref[:, sl] + ob_ref[:, sl], g_ref[...])
        o_ref[:, sl] = (o * _silu(z_ref[:, sl])).astype(BF16)


def _merge_kernel(*refs):
    b_refs, w_refs, g_refs, o_ref = refs[0:4], refs[4:8], refs[8:12], refs[12]
    acc = None
    for b_ref, w_ref, g_ref in zip(b_refs, w_refs, g_refs):
        term = jax.nn.sigmoid(g_ref[...].astype(F32)) * _dot(b_ref[...], w_ref[...])
        acc = term if acc is None else acc + term
    o_ref[...] = acc.astype(BF16)


def _pack_halves(y):
    n = y.shape[1] // 2
    bits = lax.bitcast_convert_type(y.astype(BF16).astype(F32), jnp.uint32)
    return (bits[:, :n] >> 16) | (bits[:, n:] & jnp.uint32(0xFFFF0000))


def _unpack_halves(w):
    lo = lax.bitcast_convert_type(w << 16, F32)
    hi = lax.bitcast_convert_type(w & jnp.uint32(0xFFFF0000), F32)
    return lo, hi


def _ffn_norm_router_kernel(x_ref, g_ref, sc_ref, sh_ref, rw_ref, rb_ref,
                            h_ref, hp_ref, idx_ref, rank_ref, wts_ref, cnt_ref, carry, *, n_experts):
    i = pl.program_id(0)

    @pl.when(i == 0)
    def _():
        carry[...] = jnp.zeros(carry.shape, F32)

    h = _rms(x_ref[...], g_ref[...]) * (1.0 + sc_ref[...]) + sh_ref[...]
    h_ref[...] = h.astype(BF16)
    hp_ref[...] = _pack_halves(h)
    tm = h.shape[0]
    scores = jax.nn.sigmoid(_dot_split(*_split_bf16(h), *_split_bf16(rw_ref[...])))
    lane = lax.broadcasted_iota(jnp.int32, scores.shape, 1)
    biased = jnp.where(lane < n_experts, scores + rb_ref[...], NEG_INF)
    chosen = []
    for _ in range(TOP_K):
        m = jnp.max(biased, axis=-1, keepdims=True)
        pick = jnp.min(jnp.where(biased == m, lane, LANES), axis=-1, keepdims=True)
        sel = lane == pick
        chosen.append((pick, sel))
        biased = jnp.where(sel, NEG_INF, biased)
    mask = functools.reduce(jnp.logical_or, [sel for _, sel in chosen])
    maskf = jnp.where(mask, 1.0, 0.0)
    picked = scores * maskf
    dense_w = picked / jnp.sum(picked, axis=-1, keepdims=True) * ROUTE_SCALE
    row = lax.broadcasted_iota(jnp.int32, (tm, tm), 0)
    col = lax.broadcasted_iota(jnp.int32, (tm, tm), 1)
    before = jnp.where(row > col, 1.0, 0.0).astype(BF16)
    rank = _dot(before, maskf.astype(BF16)) + carry[0:1, :]
    carry[...] = carry[...] + jnp.sum(maskf, axis=0, keepdims=True)
    idx8 = jnp.zeros(scores.shape, jnp.int32)
    rank8 = jnp.zeros(scores.shape, F32)
    wts8 = jnp.zeros(scores.shape, F32)
    for k, (pick, sel) in enumerate(chosen):
        here = lane == k
        idx8 = jnp.where(here, pick, idx8)
        rank8 = jnp.where(here, jnp.sum(jnp.where(sel, rank, 0.0), axis=-1, keepdims=True), rank8)
        wts8 = jnp.where(here, jnp.sum(jnp.where(sel, dense_w, 0.0), axis=-1, keepdims=True), wts8)
    idx_ref[...] = idx8
    rank_ref[...] = rank8.astype(jnp.int32)
    wts_ref[...] = wts8
    cnt_ref[...] = carry[...]


def _dispatch_kernel(tail_ref, has_ref, slot_hbm, x_ref, xs_hbm, slots, zbuf, sem_s, sem_z, sem, *, max_unused):
    i = pl.program_id(0)
    tm = x_ref.shape[0]
    bm = zbuf.shape[0]
    n_exp = has_ref.shape[0] - 1
    n_blocks = xs_hbm.shape[0] // bm

    @pl.when(i == 0)
    def _():
        zbuf[...] = jnp.zeros(zbuf.shape, zbuf.dtype)

        def zero_block(row0):
            return pltpu.make_async_copy(zbuf, xs_hbm.at[pl.ds(row0, bm)], sem_z)

        n_used = has_ref[n_exp]
        for start in (True, False):
            for e in range(n_exp):
                cp = zero_block(pl.multiple_of(tail_ref[e], bm))
                pl.when(has_ref[e] == 1)(cp.start if start else cp.wait)
            for k in range(max_unused):
                cp = zero_block((n_blocks - 1 - k) * bm)
                pl.when(n_blocks - 1 - k >= n_used)(cp.start if start else cp.wait)

    cp = pltpu.make_async_copy(slot_hbm.at[i], slots, sem_s)
    cp.start()
    cp.wait()

    def row_refs(r, j):
        return x_ref.at[pl.ds(r, 1)], xs_hbm.at[pl.ds(slots[r * TOP_K + j], 1)]

    def issue(r, carry):
        for j in range(TOP_K):
            pltpu.make_async_copy(*row_refs(r, j), sem).start()
        return carry

    lax.fori_loop(0, tm, issue, 0)

    def drain(r, carry):
        for j in range(TOP_K):
            pltpu.make_async_copy(*row_refs(r, j), sem).wait()
        return carry

    lax.fori_loop(0, tm, drain, 0)


def _expert_kernel(be_ref, used_ref, xblk_ref, x_ref, wgu_ref, wd_ref, o_ref):
    del be_ref, xblk_ref
    b = pl.program_id(0)
    f = wd_ref.shape[0]
    half = x_ref.shape[1]

    @pl.when(used_ref[b] == 1)
    def _():
        x_lo, x_hi = _unpack_halves(x_ref[...])
        gu = _dot(x_lo.astype(BF16), wgu_ref[:half, :]) + _dot(x_hi.astype(BF16), wgu_ref[half:, :])
        act = (_silu(gu[:, :f]) * gu[:, f:]).astype(BF16)
        o_ref[...] = _pack_halves(_dot(act, wd_ref[...]))

    @pl.when(used_ref[b] == 0)
    def _():
        o_ref[...] = jnp.zeros(o_ref.shape, jnp.uint32)


def _combine_kernel(slot_hbm, ys_hbm, wts_ref, o_ref, slots, buf, sem_s, sem):
    i = pl.program_id(0)
    tm = o_ref.shape[0]
    half = buf.shape[2]
    cs = pltpu.make_async_copy(slot_hbm.at[i], slots, sem_s)
    cs.start()
    cs.wait()

    def row_refs(r, j):
        return ys_hbm.at[pl.ds(slots[r * TOP_K + j], 1)], buf.at[j, pl.ds(r, 1)]

    def issue(r, carry):
        for j in range(TOP_K):
            pltpu.make_async_copy(*row_refs(r, j), sem).start()
        return carry

    lax.fori_loop(0, tm, issue, 0)

    def drain(r, carry):
        for j in range(TOP_K):
            pltpu.make_async_copy(*row_refs(r, j), sem).wait()
        return carry

    lax.fori_loop(0, tm, drain, 0)

    acc_lo = acc_hi = None
    for j in range(TOP_K):
        lo, hi = _unpack_halves(buf[j])
        wj = wts_ref[:, j:j + 1]
        acc_lo = wj * lo if acc_lo is None else acc_lo + wj * lo
        acc_hi = wj * hi if acc_hi is None else acc_hi + wj * hi
    o_ref[:, :half] = acc_lo
    o_ref[:, half:] = acc_hi


def _shared_up_kernel(h_ref, w_ref, o_ref):
    gu = _dot(h_ref[...], w_ref[...])
    f = gu.shape[1] // 2
    o_ref[...] = (_silu(gu[:, :f]) * gu[:, f:]).astype(BF16)


def _rope_tables(n_tok, n_rot, lo, width, n_identity):
    rows = n_tok // GRID_W
    row = jnp.repeat(jnp.arange(rows, dtype=F32), GRID_W)
    col = jnp.tile(jnp.arange(GRID_W, dtype=F32), rows)
    half = n_rot // 2
    inv_freq = ROPE_THETA ** (-jnp.arange(0, half, 2, dtype=F32) / half)
    ar, ac = row[:, None] * inv_freq[None, :], col[:, None] * inv_freq[None, :]
    cos = jnp.concatenate([jnp.cos(ar), jnp.cos(ar), jnp.cos(ac), jnp.cos(ac)], axis=1)
    sin = jnp.concatenate([-jnp.sin(ar), jnp.sin(ar), -jnp.sin(ac), jnp.sin(ac)], axis=1)
    cos = jnp.pad(cos, ((0, n_identity), (lo, width - lo - n_rot)), constant_values=1.0)
    cos = cos.at[n_tok:, :].set(1.0)
    sin = jnp.pad(sin, ((0, n_identity), (lo, width - lo - n_rot)))
    return cos, sin


def kernel(x_prompt, x_sample, cache_mla_kv, cache_mla_kr, cache_gqa_k, cache_gqa_v, state_dn, c, c_ctx,
           w_mod, b_mod, norm_mix, norm_ffn, w_in, mla_q_norm, mla_w_q_up, mla_kv_norm, mla_w_kv_up,
           conv_dw_w, conv_dw_b, conv_ln_w, conv_ln_b, gqa_q_norm, gqa_k_norm, dn_conv_w, dn_a_log,
           dn_dt_bias, dn_o_norm, w_branch, w_out, router_w, router_bias, exp_w_gu, exp_w_down,
           sh_w_gu, sh_w_down, final_norm):
    nb, seq, d = x_prompt.shape
    db, dseq, _ = x_sample.shape
    depth = w_mod.shape[0]
    past = cache_mla_kv.shape[2]
    n_exp, _, f2 = exp_w_gu.shape[1:]
    fexp = f2 // 2
    tp, ts = nb * seq, db * dseq
    t = tp + ts
    tm = SEQ_TILE
    assert seq == tm and past == tm and db + 1 <= 8 and tp % (2 * tm) == 0 and dseq % (2 * tm) == 0 and tp % dseq == 0
    npt, spt = tp // tm, dseq // tm
    n_tiles = t // tm

    def row_of(i):
        return jnp.where(i < npt, 0, 1 + (i - npt) // spt)

    def pos_of(i):
        return jnp.where(i < npt, spt, (i - npt) % spt)

    tile_ids = np.arange(n_tiles)
    in_sample = tile_ids >= npt
    edges = jnp.asarray(np.stack([np.where(in_sample, (tile_ids - npt) % spt == 0, True),
                                  np.where(in_sample, (tile_ids - npt) % spt == spt - 1, True)]).astype(np.int32))

    x = jnp.concatenate([x_prompt.reshape(tp, d), x_sample.reshape(ts, d)], axis=0)
    cond8 = jnp.zeros((8, d), F32).at[0].set(c_ctx).at[1:1 + db].set(c)

    cos_m, sin_m = _rope_tables(dseq, MLA_ROPE, MLA_NOPE, HEAD_W, tm)
    cos_k, sin_k = _rope_tables(dseq, MLA_ROPE, 0, LANES, tm)
    cos_g, sin_g = _rope_tables(dseq, GQA_HD, 0, GQA_HD, tm)

    sizes = (Q_LORA, KV_LORA, MLA_ROPE, 2 * CONV_CH, GQA_HEADS * GQA_HD, GQA_KV_HEADS * GQA_HD,
             GQA_KV_HEADS * GQA_HD, DN_HEADS * DN_DK, DN_HEADS * DN_DK, DN_HEADS * DN_DV, DN_HEADS * DN_DV,
             2 * DN_HEADS, 2 * DN_HEADS, N_BRANCH * d)
    names = ("q_c", "kv_c", "k_r", "glu", "g_q", "g_k", "g_v", "d_q", "d_k", "d_v", "d_z", "d_a", "d_b", "gates")
    src = dict(zip(names, zip(np.cumsum((0,) + sizes[:-1]).tolist(), sizes)))
    order = ("glu", "g_q", "q_c", "g_k", "d_q", "d_k", "d_v", "d_z", "kv_c", "g_v", "k_r", "d_ab")
    col = {}
    off = 0
    src["d_ab"] = (src["d_a"][0], 4 * DN_HEADS)
    for name in order:
        width = max(LANES, src[name][1])
        off = -(-off // width) * width
        col[name] = (off, width)
        off += width
    p_width = -(-off // 512) * 512

    def arrange_w_in(w):
        parts, pos = [], 0
        for name in order:
            a, n = src[name]
            parts.append(jnp.pad(w[:, a:a + n], ((0, 0), (col[name][0] - pos, col[name][1] - n))))
            pos = col[name][0] + col[name][1]
        parts.append(jnp.zeros((w.shape[0], p_width - pos), w.dtype))
        return jnp.concatenate(parts, axis=1).astype(BF16)

    def head_pad(w, n_in, widths, pads):
        h = w.shape[1] // sum(widths)
        w3 = w.reshape(n_in, h, sum(widths))
        parts, a = [], 0
        for wd, pd in zip(widths, pads):
            parts.append(jnp.pad(w3[:, :, a:a + wd], ((0, 0), (0, 0), (0, pd))))
            a += wd
        return jnp.concatenate(parts, axis=2).reshape(n_in, -1).astype(BF16)

    def pcol(name, width=None):
        a, n = col[name]
        width = n if width is None else width
        assert a % width == 0
        return a // width

    w_out_b, w_branch_b = w_out.astype(BF16), w_branch.astype(BF16)
    sh_gu_b, sh_dw_b = sh_w_gu.astype(BF16), sh_w_down.astype(BF16)
    wkv_b = mla_w_kv_up.astype(BF16)

    kv_lats, krs, gks, gvs, states = [], [], [], [], []
    for l in range(depth):
        mod4 = _modulation(cond8, w_mod, b_mod, l).reshape(8, 6, 1, d)
        h = _norm_mod(x, norm_mix, l, mod4, 1, 0, row_of)
        proj = _matmul(h, arrange_w_in(w_in[l])[None], 0, F32, "in_proj")
        ga, gn = src["gates"]
        gates = _matmul(h, w_in[l][:, ga:ga + gn].astype(BF16)[None], 0, BF16, "in_proj_gates")

        wq = head_pad(mla_w_q_up[l], Q_LORA, (MLA_NOPE, MLA_ROPE), (0, HEAD_W - MLA_QK))
        q_mla = pl.pallas_call(
            _mla_q_kernel,
            out_shape=jax.ShapeDtypeStruct((t, MLA_HEADS * HEAD_W), BF16),
            grid=(n_tiles,),
            in_specs=[pl.BlockSpec((tm, Q_LORA), lambda i: (i, pcol("q_c"))),
                      pl.BlockSpec((None, 1, Q_LORA), lambda i: (l, 0, 0)),
                      pl.BlockSpec((Q_LORA, MLA_HEADS * HEAD_W), lambda i: (0, 0)),
                      pl.BlockSpec((tm, HEAD_W), lambda i: (pos_of(i), 0)),
                      pl.BlockSpec((tm, HEAD_W), lambda i: (pos_of(i), 0))],
            out_specs=pl.BlockSpec((tm, MLA_HEADS * HEAD_W), lambda i: (i, 0)),
            compiler_params=_params(("parallel",), 32),
            name="mla_q",
        )(proj, mla_q_norm.reshape(depth, 1, Q_LORA), wq, cos_m, sin_m)
        kv_lat, kv_up, kr_rot = pl.pallas_call(
            _mla_kv_kernel,
            out_shape=[jax.ShapeDtypeStruct((t, KV_LORA), F32),
                       jax.ShapeDtypeStruct((t, MLA_HEADS * HEAD_W), BF16),
                       jax.ShapeDtypeStruct((t, LANES), BF16)],
            grid=(n_tiles,),
            in_specs=[pl.BlockSpec((tm, KV_LORA), lambda i: (i, pcol("kv_c"))),
                      pl.BlockSpec((None, 1, KV_LORA), lambda i: (l, 0, 0)),
                      pl.BlockSpec((None, KV_LORA, MLA_HEADS * HEAD_W), lambda i: (l, 0, 0)),
                      pl.BlockSpec((tm, LANES), lambda i: (i, pcol("k_r"))),
                      pl.BlockSpec((tm, LANES), lambda i: (pos_of(i), 0)),
                      pl.BlockSpec((tm, LANES), lambda i: (pos_of(i), 0))],
            out_specs=[pl.BlockSpec((tm, KV_LORA), lambda i: (i, 0)),
                       pl.BlockSpec((tm, MLA_HEADS * HEAD_W), lambda i: (i, 0)),
                       pl.BlockSpec((tm, LANES), lambda i: (i, 0))],
            compiler_params=_params(("parallel",), 32),
            name="mla_kv",
        )(proj, mla_kv_norm.reshape(depth, 1, KV_LORA), wkv_b, proj, cos_k, sin_k)
        kv_up_c = _matmul(cache_mla_kv[:, l].reshape(db * past, KV_LORA).astype(BF16), wkv_b, l, BF16,
                          "mla_kv_cache")
        kr_c = jnp.pad(cache_mla_kr[:, l].reshape(db * past, MLA_ROPE), ((0, 0), (0, LANES - MLA_ROPE))).astype(BF16)
        o_a = jnp.concatenate([
            _attention(_mla_attn_kernel, q_mla, (kv_up, kr_rot), None, MLA_HEADS * MLA_V, nb, seq, 0, past,
                       "mla_attn_ctx"),
            _attention(_mla_attn_kernel, q_mla, (kv_up, kr_rot), (kv_up_c, kr_c), MLA_HEADS * MLA_V, db, dseq, tp,
                       past, "mla_attn_lat")], axis=0)

        halo_b = 16
        hb = tm // halo_b
        o_b = pl.pallas_call(
            _conformer_kernel,
            out_shape=jax.ShapeDtypeStruct((t, CONV_CH), BF16),
            grid_spec=pltpu.PrefetchScalarGridSpec(
                num_scalar_prefetch=1,
                grid=(n_tiles,),
                in_specs=[pl.BlockSpec((tm, 2 * CONV_CH), lambda i, e: (i, pcol("glu"))),
                          pl.BlockSpec((halo_b, 2 * CONV_CH),
                                       lambda i, e: (jnp.maximum(i * hb - 1, 0), pcol("glu"))),
                          pl.BlockSpec((halo_b, 2 * CONV_CH),
                                       lambda i, e: (jnp.minimum((i + 1) * hb, n_tiles * hb - 1), pcol("glu"))),
                          pl.BlockSpec((None, CONV_K, CONV_CH), lambda i, e: (l, 0, 0)),
                          pl.BlockSpec((None, 1, CONV_CH), lambda i, e: (l, 0, 0)),
                          pl.BlockSpec((None, 1, CONV_CH), lambda i, e: (l, 0, 0)),
                          pl.BlockSpec((None, 1, CONV_CH), lambda i, e: (l, 0, 0))],
                out_specs=pl.BlockSpec((tm, CONV_CH), lambda i, e: (i, 0)),
                scratch_shapes=[pltpu.VMEM((tm + 2 * halo_b, CONV_CH), F32), pltpu.VMEM((tm, CONV_CH), F32)]),
            compiler_params=_params(("parallel",), 32),
            name="conformer_conv",
        )(edges, proj, proj, proj, conv_dw_w, conv_dw_b.reshape(depth, 1, CONV_CH),
          conv_ln_w.reshape(depth, 1, CONV_CH), conv_ln_b.reshape(depth, 1, CONV_CH))

        wq_g, wk_g = GQA_HEADS * GQA_HD, GQA_KV_HEADS * GQA_HD
        q_g, k_g, v_g, k_norm = pl.pallas_call(
            _gqa_prep_kernel,
            out_shape=[jax.ShapeDtypeStruct((t, wq_g), BF16), jax.ShapeDtypeStruct((t, wk_g), BF16),
                       jax.ShapeDtypeStruct((t, wk_g), BF16), jax.ShapeDtypeStruct((t, wk_g), F32)],
            grid=(n_tiles,),
            in_specs=[pl.BlockSpec((tm, wq_g), lambda i: (i, pcol("g_q"))),
                      pl.BlockSpec((tm, wk_g), lambda i: (i, pcol("g_k"))),
                      pl.BlockSpec((tm, wk_g), lambda i: (i, pcol("g_v"))),
                      pl.BlockSpec((None, 1, GQA_HD), lambda i: (l, 0, 0)),
                      pl.BlockSpec((None, 1, GQA_HD), lambda i: (l, 0, 0)),
                      pl.BlockSpec((tm, GQA_HD), lambda i: (pos_of(i), 0)),
                      pl.BlockSpec((tm, GQA_HD), lambda i: (pos_of(i), 0))],
            out_specs=[pl.BlockSpec((tm, wq_g), lambda i: (i, 0)), pl.BlockSpec((tm, wk_g), lambda i: (i, 0)),
                       pl.BlockSpec((tm, wk_g), lambda i: (i, 0)), pl.BlockSpec((tm, wk_g), lambda i: (i, 0))],
            compiler_params=_params(("parallel",), 32),
            name="gqa_prep",
        )(proj, proj, proj, gqa_q_norm.reshape(depth, 1, GQA_HD), gqa_k_norm.reshape(depth, 1, GQA_HD),
          cos_g, sin_g)
        kc_g = cache_gqa_k[:, l].reshape(db * past, wk_g).astype(BF16)
        vc_g = cache_gqa_v[:, l].reshape(db * past, wk_g).astype(BF16)
        o_c = jnp.concatenate([
            _attention(_gqa_attn_kernel, q_g, (k_g, v_g), None, wq_g, nb, seq, 0, past, "gqa_attn_ctx"),
            _attention(_gqa_attn_kernel, q_g, (k_g, v_g), (kc_g, vc_g), wq_g, db, dseq, tp, past,
                       "gqa_attn_lat")], axis=0)

        halo_d = 8
        hd = tm // halo_d
        wdn = DN_HEADS * DN_DK
        prev_map = lambda name: (lambda i, e: (jnp.maximum(i * hd - 1, 0), pcol(name)))
        next_map = lambda name: (lambda i, e: (jnp.minimum((i + 1) * hd, n_tiles * hd - 1), pcol(name)))
        main_specs = [pl.BlockSpec((tm, wdn), (lambda name: (lambda i, e: (i, pcol(name))))(nm))
                      for nm in ("d_q", "d_k", "d_v")]
        prev_specs = [pl.BlockSpec((halo_d, wdn), prev_map(nm)) for nm in ("d_q", "d_k", "d_v")]
        next_specs = [pl.BlockSpec((halo_d, wdn), next_map(nm)) for nm in ("d_q", "d_k", "d_v")]
        pad16 = lambda a: jnp.pad(a.reshape(1, 2 * DN_HEADS), ((0, 0), (0, LANES - 2 * DN_HEADS)))
        qd, kd, vd, gbeta = pl.pallas_call(
            _dn_prep_kernel,
            out_shape=[jax.ShapeDtypeStruct((t, wdn), F32)] * 3 + [jax.ShapeDtypeStruct((t, LANES), F32)],
            grid_spec=pltpu.PrefetchScalarGridSpec(
                num_scalar_prefetch=1,
                grid=(n_tiles,),
                in_specs=main_specs + prev_specs + next_specs + [
                    pl.BlockSpec((tm, LANES), lambda i, e: (i, pcol("d_ab"))),
                    pl.BlockSpec((None, DN_CONV, 3 * wdn), lambda i, e: (l, 0, 0)),
                    pl.BlockSpec((1, LANES), lambda i, e: (0, 0)),
                    pl.BlockSpec((1, LANES), lambda i, e: (0, 0))],
                out_specs=[pl.BlockSpec((tm, wdn), lambda i, e: (i, 0))] * 3
                + [pl.BlockSpec((tm, LANES), lambda i, e: (i, 0))],
                scratch_shapes=[pltpu.VMEM((tm + 2 * halo_d, wdn), F32)]),
            compiler_params=_params(("parallel",), 40),
            name="dn_prep",
        )(edges, *([proj] * 10), dn_conv_w, pad16(dn_a_log[l]), pad16(dn_dt_bias[l]))
        of_p, ob_p, st_p, wgu_l = _dn_scan(qd, kd, vd, gbeta, None, exp_w_gu.reshape(depth, n_exp * d, f2), l,
                                           nb, seq, 0, True, "dn_scan_ctx")
        s0 = state_dn[:, l].reshape(db, 2 * DN_HEADS, DN_DK, DN_DV)
        of_s, ob_s, wdw_l = _dn_scan(qd, kd, vd, gbeta, s0, exp_w_down.reshape(depth, n_exp * fexp, d), l,
                                     db, dseq, tp, False, "dn_scan_lat")
        wgu_l, wdw_l = wgu_l.reshape(n_exp, d, f2), wdw_l.reshape(n_exp, fexp, d)
        o_f = jnp.concatenate([of_p, of_s], axis=0)
        o_bw = jnp.concatenate([ob_p, ob_s], axis=0)
        o_d = pl.pallas_call(
            _dn_post_kernel,
            out_shape=jax.ShapeDtypeStruct((t, wdn), BF16),
            grid=(n_tiles,),
            in_specs=[pl.BlockSpec((tm, wdn), lambda i: (i, 0)), pl.BlockSpec((tm, wdn), lambda i: (i, 0)),
                      pl.BlockSpec((tm, wdn), lambda i: (i, pcol("d_z"))),
                      pl.BlockSpec((None, 1, DN_DV), lambda i: (l, 0, 0))],
            out_specs=pl.BlockSpec((tm, wdn), lambda i: (i, 0)),
            compiler_params=_params(("parallel",), 32),
            name="dn_post",
        )(o_f, o_bw, proj, dn_o_norm.reshape(depth, 1, DN_DV))

        tmm, tnm = _pick(t, 1024), _pick(d, 512)
        g0 = 0
        wb = w_branch_b
        branch_specs = [pl.BlockSpec((tmm, BRANCH_W), lambda i, j: (i, 0))] * N_BRANCH
        w_specs = [pl.BlockSpec((None, None, BRANCH_W, tnm), (lambda b: (lambda i, j: (l, b, 0, j)))(b))
                   for b in range(N_BRANCH)]
        gate_specs = [pl.BlockSpec((tmm, tnm), (lambda b: (lambda i, j: (i, g0 + b * (d // tnm) + j)))(b))
                      for b in range(N_BRANCH)]
        merged = pl.pallas_call(
            _merge_kernel,
            out_shape=jax.ShapeDtypeStruct((t, d), BF16),
            grid=(t // tmm, d // tnm),
            in_specs=branch_specs + w_specs + gate_specs,
            out_specs=pl.BlockSpec((tmm, tnm), lambda i, j: (i, j)),
            compiler_params=_params(("parallel", "parallel"), 56),
            name="branch_merge",
        )(o_a, o_b, o_c, o_d, wb, wb, wb, wb, gates, gates, gates, gates)
        x = _matmul_residual(merged, w_out_b, l, x, mod4, 2, row_of, None, "out_proj")

        n_pad = LANES - n_exp
        rw = jnp.pad(router_w[l], ((0, 0), (0, n_pad)))
        rb = jnp.pad(router_bias[l].reshape(1, n_exp), ((0, 0), (0, n_pad)))
        half = d // 2
        h2, h2_packed, idx8, rank8, wts8, counts = pl.pallas_call(
            functools.partial(_ffn_norm_router_kernel, n_experts=n_exp),
            out_shape=[jax.ShapeDtypeStruct((t, d), BF16), jax.ShapeDtypeStruct((t, half), jnp.uint32),
                       jax.ShapeDtypeStruct((t, LANES), jnp.int32),
                       jax.ShapeDtypeStruct((t, LANES), jnp.int32), jax.ShapeDtypeStruct((t, LANES), F32),
                       jax.ShapeDtypeStruct((8, LANES), F32)],
            grid=(n_tiles,),
            in_specs=[pl.BlockSpec((tm, d), lambda i: (i, 0)),
                      pl.BlockSpec((None, 1, d), lambda i: (l, 0, 0)),
                      pl.BlockSpec((None, None, 1, d), lambda i: (row_of(i), 4, 0, 0)),
                      pl.BlockSpec((None, None, 1, d), lambda i: (row_of(i), 3, 0, 0)),
                      pl.BlockSpec((d, LANES), lambda i: (0, 0)),
                      pl.BlockSpec((1, LANES), lambda i: (0, 0))],
            out_specs=[pl.BlockSpec((tm, d), lambda i: (i, 0)), pl.BlockSpec((tm, half), lambda i: (i, 0)),
                       pl.BlockSpec((tm, LANES), lambda i: (i, 0)),
                       pl.BlockSpec((tm, LANES), lambda i: (i, 0)), pl.BlockSpec((tm, LANES), lambda i: (i, 0)),
                       pl.BlockSpec((8, LANES), lambda i: (0, 0))],
            scratch_shapes=[pltpu.VMEM((8, LANES), F32)],
            compiler_params=_params(("arbitrary",), 48),
            name="ffn_norm_router",
        )(x, norm_ffn.reshape(depth, 1, d), mod4, mod4, rw, rb)

        bm = EXPERT_BLOCK
        n_blocks = (t * TOP_K + n_exp * (bm - 1) + bm - 1) // bm
        cnt = counts[0, :n_exp].astype(jnp.int32)
        padded = (cnt + bm - 1) // bm * bm
        pad_end = jnp.cumsum(padded)
        pad_start = pad_end - padded
        slot8 = (pad_start[idx8[:, :TOP_K]] + rank8[:, :TOP_K]).astype(jnp.int32)
        blk = jnp.arange(n_blocks, dtype=jnp.int32)
        block_e = jnp.minimum(jnp.sum((pad_end[None, :] <= blk[:, None] * bm).astype(jnp.int32), axis=1),
                              n_exp - 1).astype(jnp.int32)
        n_used = pad_end[-1] // bm
        used = (blk < n_used).astype(jnp.int32)
        x_blk = jnp.minimum(blk, n_used - 1).astype(jnp.int32)
        tail = (pad_end - bm).astype(jnp.int32)
        has = jnp.concatenate([(padded > 0).astype(jnp.int32), n_used.reshape(1).astype(jnp.int32)])
        max_unused = n_blocks - (t * TOP_K + bm - 1) // bm

        n_slots = n_blocks * bm
        xs = pl.pallas_call(
            functools.partial(_dispatch_kernel, max_unused=max_unused),
            out_shape=jax.ShapeDtypeStruct((n_slots, half), jnp.uint32),
            grid_spec=pltpu.PrefetchScalarGridSpec(
                num_scalar_prefetch=2,
                grid=(n_tiles,),
                in_specs=[pl.BlockSpec(memory_space=pl.ANY),
                          pl.BlockSpec((tm, half), lambda i, tl, hs: (i, 0))],
                out_specs=pl.BlockSpec(memory_space=pl.ANY),
                scratch_shapes=[pltpu.SMEM((tm * TOP_K,), jnp.int32), pltpu.VMEM((bm, half), jnp.uint32),
                                pltpu.SemaphoreType.DMA, pltpu.SemaphoreType.DMA, pltpu.SemaphoreType.DMA]),
            compiler_params=_params(("arbitrary",), 32),
            name="moe_dispatch",
        )(tail, has, slot8.reshape(n_tiles, tm * TOP_K), h2_packed)

        ys = pl.pallas_call(
            _expert_kernel,
            out_shape=jax.ShapeDtypeStruct((n_slots, half), jnp.uint32),
            grid_spec=pltpu.PrefetchScalarGridSpec(
                num_scalar_prefetch=3,
                grid=(n_blocks,),
                in_specs=[pl.BlockSpec((bm, half), lambda b, be, us, xb: (xb[b], 0)),
                          pl.BlockSpec((None, d, f2), lambda b, be, us, xb: (be[b], 0, 0)),
                          pl.BlockSpec((None, fexp, d), lambda b, be, us, xb: (be[b], 0, 0))],
                out_specs=pl.BlockSpec((bm, half), lambda b, be, us, xb: (b, 0))),
            compiler_params=_params(("arbitrary",), 48),
            name="moe_experts",
        )(block_e, used, x_blk, xs, wgu_l, wdw_l)

        tmc = 128
        y_routed = pl.pallas_call(
            _combine_kernel,
            out_shape=jax.ShapeDtypeStruct((t, d), F32),
            grid=(t // tmc,),
            in_specs=[pl.BlockSpec(memory_space=pl.ANY), pl.BlockSpec(memory_space=pl.ANY),
                      pl.BlockSpec((tmc, LANES), lambda i: (i, 0))],
            out_specs=pl.BlockSpec((tmc, d), lambda i: (i, 0)),
            scratch_shapes=[pltpu.SMEM((tmc * TOP_K,), jnp.int32),
                            pltpu.VMEM((TOP_K, tmc, half), jnp.uint32),
                            pltpu.SemaphoreType.DMA, pltpu.SemaphoreType.DMA],
            compiler_params=_params(("arbitrary",), 40),
            name="moe_combine",
        )(slot8.reshape(t // tmc, tmc * TOP_K), ys, wts8)

        tms = _pick(t, 512)
        f2s = sh_w_gu.shape[2]
        act_sh = pl.pallas_call(
            _shared_up_kernel,
            out_shape=jax.ShapeDtypeStruct((t, f2s // 2), BF16),
            grid=(t // tms,),
            in_specs=[pl.BlockSpec((tms, d), lambda i: (i, 0)),
                      pl.BlockSpec((None, d, f2s), lambda i: (l, 0, 0))],
            out_specs=pl.BlockSpec((tms, f2s // 2), lambda i: (i, 0)),
            compiler_params=_params(("parallel",), 40),
            name="shared_up",
        )(h2, sh_gu_b)
        x = _matmul_residual(act_sh, sh_dw_b, l, x, mod4, 5, row_of, y_routed, "ffn_out")

        kv_lats.append(kv_lat[:tp].reshape(nb, seq, KV_LORA))
        a, n = col["k_r"][0], MLA_ROPE
        krs.append(proj[:tp, a:a + n].reshape(nb, seq, MLA_ROPE))
        gks.append(k_norm[:tp].reshape(nb, seq, GQA_KV_HEADS, GQA_HD))
        a, n = col["g_v"]
        gvs.append(proj[:tp, a:a + n].reshape(nb, seq, GQA_KV_HEADS, GQA_HD))
        states.append(st_p.reshape(nb, 2, DN_HEADS, DN_DK, DN_DV))

    y = _final_norm(x, final_norm)
    return (y[:tp].reshape(nb, seq, d), y[tp:].reshape(db, dseq, d),
            jnp.stack(kv_lats, axis=1), jnp.stack(krs, axis=1), jnp.stack(gks, axis=1),
            jnp.stack(gvs, axis=1), jnp.stack(states, axis=1))
```
